```python
import functools
import jax, jax.numpy as jnp
from jax import lax
import numpy as np

D_MODEL = 2048
BATCH = 16
SEQ = 2048
DEPTH = 1
DEC_BATCH = 32
DEC_SEQ = 1
PAST_LEN = 16384
PAGE_SIZE = 128

D_CONV = D_MODEL // 2
CONV_W = 3
N_HEADS = 16
HEAD_DIM = (D_MODEL // 2) // N_HEADS
D_ATTN = N_HEADS * HEAD_DIM
D_MIX = D_CONV + D_ATTN
D_FF = 4 * D_MODEL
Q_BLOCK = 128
NORM_EPS = 1e-6
FORGET_BIAS_INIT = 3.0
CACHE_FORGET_BIAS = 9.0
D_IN = 3 * D_CONV + 3 * D_ATTN + N_HEADS + 2 * D_MODEL

kernel_name = "fox_shortconv_hybrid_decoder_step"


def _rmsnorm(x, g):
    x32 = x.astype(jnp.float32)
    y = x32 * lax.rsqrt(jnp.mean(x32 * x32, axis=-1, keepdims=True) + NORM_EPS)
    return (y * g.astype(jnp.float32)).astype(x.dtype)


def _modulation(c, w_ada, b_ada):
    mod = jax.nn.silu(c) @ w_ada + b_ada
    return [m[:, None, :] for m in jnp.split(mod, 6, axis=-1)]


def _conv_mixer(cb, cc, cx, hist, conv_w):
    z = cc * cx
    zh = jnp.concatenate([hist.astype(z.dtype), z], axis=1)
    t = z.shape[1]
    y = conv_w[0] * zh[:, 0:t]
    for i in range(1, CONV_W):
        y = y + conv_w[i] * zh[:, i:i + t]
    return cb * y, zh[:, -(CONV_W - 1):]


def _fox_prompt(q, k, v, logf):
    n, s_len = q.shape[0], q.shape[1]
    scale = HEAD_DIM ** -0.5
    c_h = jnp.cumsum(logf, axis=1).transpose(0, 2, 1)
    nb = s_len // Q_BLOCK
    qb = q.reshape(n, nb, Q_BLOCK, N_HEADS, HEAD_DIM).transpose(1, 0, 2, 3, 4)
    cqb = c_h.reshape(n, N_HEADS, nb, Q_BLOCK).transpose(2, 0, 1, 3)
    kpos = jnp.arange(s_len)

    def block(args):
        q_i, cq_i, i = args
        s = jnp.einsum('bqhd,bkhd->bhqk', q_i, k, preferred_element_type=jnp.float32) * scale
        s = s + cq_i[..., None] - c_h[:, :, None, :]
        qpos = i * Q_BLOCK + jnp.arange(Q_BLOCK)
        s = jnp.where(qpos[:, None] >= kpos[None, :], s, -jnp.inf)
        p = jax.nn.softmax(s, axis=-1).astype(v.dtype)
        return jnp.einsum('bhqk,bkhd->bqhd', p, v)

    o = lax.map(block, (qb, cqb, jnp.arange(nb)))
    return o.transpose(1, 0, 2, 3, 4).reshape(n, s_len, D_ATTN)


def _fox_sample(q, k, v, logf, cache_k, cache_v, cache_logf, page_table):
    nd, n = q.shape[0], q.shape[1]
    scale = HEAD_DIM ** -0.5
    p_len = page_table.shape[1] * cache_k.shape[1]
    k_past = cache_k[page_table].reshape(nd, p_len, N_HEADS, HEAD_DIM)
    v_past = cache_v[page_table].reshape(nd, p_len, N_HEADS, HEAD_DIM)
    lf_past = cache_logf[page_table].reshape(nd, p_len, N_HEADS).astype(jnp.float32)
    rc = lax.cumsum(lf_past, axis=1, reverse=True)
    suffix = jnp.concatenate([rc[:, 1:], jnp.zeros_like(rc[:, :1])], axis=1)
    cn = jnp.cumsum(logf, axis=1).transpose(0, 2, 1)
    s_past = jnp.einsum('bqhd,bkhd->bhqk', q, k_past, preferred_element_type=jnp.float32) * scale
    s_past = s_past + cn[..., None] + suffix.transpose(0, 2, 1)[:, :, None, :]
    s_new = jnp.einsum('bqhd,bkhd->bhqk', q, k, preferred_element_type=jnp.float32) * scale
    s_new = s_new + cn[..., :, None] - cn[..., None, :]
    causal = jnp.arange(n)[:, None] >= jnp.arange(n)[None, :]
    s_new = jnp.where(causal, s_new, -jnp.inf)
    p = jax.nn.softmax(jnp.concatenate([s_past, s_new], axis=-1), axis=-1)
    o = (jnp.einsum('bhqk,bkhd->bqhd', p[..., :p_len].astype(v_past.dtype), v_past)
         + jnp.einsum('bhqk,bkhd->bqhd', p[..., p_len:].astype(v.dtype), v))
    return o.reshape(nd, n, D_ATTN)


def _layer(x, c, conv_hist, attend, w_ada, b_ada, g_mix, w_in, b_f, conv_w, w_out, g_mlp, w_up, w_down):
    sh_a, sc_a, gt_a, sh_m, sc_m, gt_m = _modulation(c, w_ada, b_ada)
    n, t, _ = x.shape
    u = _rmsnorm(x, g_mix) * (1 + sc_a) + sh_a
    splits = [D_CONV, 2 * D_CONV, 3 * D_CONV, 3 * D_CONV + D_ATTN, 3 * D_CONV + 2 * D_ATTN,
              3 * D_CONV + 3 * D_ATTN, 3 * D_CONV + 3 * D_ATTN + N_HEADS,
              3 * D_CONV + 3 * D_ATTN + N_HEADS + D_MODEL]
    cb, cc, cx, q, k, v, fl, ga, gb = jnp.split(u @ w_in, splits, axis=-1)
    logf = jax.nn.log_sigmoid((fl + b_f).astype(jnp.float32))
    a, conv_state = _conv_mixer(cb, cc, cx, conv_hist, conv_w)
    q = q.reshape(n, t, N_HEADS, HEAD_DIM)
    k = k.reshape(n, t, N_HEADS, HEAD_DIM)
    v = v.reshape(n, t, N_HEADS, HEAD_DIM)
    b = attend(q, k, v, logf)
    mixed = jax.nn.sigmoid(ga) * (a @ w_out[:D_CONV]) + jax.nn.sigmoid(gb) * (b @ w_out[D_CONV:])
    h = x + gt_a * mixed
    m = _rmsnorm(h, g_mlp) * (1 + sc_m) + sh_m
    h = h + gt_m * (jnp.square(jax.nn.relu(m @ w_up)) @ w_down)
    return h, k, v, logf, conv_state


def setup_inputs(seed: int = 0) -> dict:
    key = jax.random.key(seed)
    ks = jax.random.split(key, 24)
    f32 = jnp.float32
    n_pages = PAST_LEN // PAGE_SIZE
    n_used = DEC_BATCH * n_pages
    n_pool = (n_used * 5) // 4
    nrm = lambda k, shape, s: (jax.random.normal(k, shape, f32) * s).astype(f32)
    page_table = jax.random.permutation(ks[0], n_pool)[:n_used].reshape(DEC_BATCH, n_pages).astype(jnp.int32)
    return {
        "x_prompt": nrm(ks[1], (BATCH, SEQ, D_MODEL), 1.0),
        "x_sample": nrm(ks[2], (DEC_BATCH, DEC_SEQ, D_MODEL), 1.0),
        "c_prompt": nrm(ks[3], (BATCH, D_MODEL), 1.0),
        "c_sample": nrm(ks[4], (DEC_BATCH, D_MODEL), 1.0),
        "cache_k": nrm(ks[5], (DEPTH, n_pool, PAGE_SIZE, N_HEADS, HEAD_DIM), 1.0),
        "cache_v": nrm(ks[6], (DEPTH, n_pool, PAGE_SIZE, N_HEADS, HEAD_DIM), 1.0),
        "cache_logf": jax.nn.log_sigmoid(CACHE_FORGET_BIAS + nrm(ks[7], (DEPTH, n_pool, PAGE_SIZE, N_HEADS), 1.0)),
        "state_conv": nrm(ks[8], (DEPTH, DEC_BATCH, CONV_W - 1, D_CONV), 1.0),
        "page_table": page_table,
        "w_ada": nrm(ks[9], (DEPTH, D_MODEL, 6 * D_MODEL), 0.5 * D_MODEL ** -0.5),
        "b_ada": nrm(ks[10], (DEPTH, 6 * D_MODEL), 0.02),
        "g_mix": 1.0 + nrm(ks[11], (DEPTH, D_MODEL), 0.02),
        "w_in": nrm(ks[12], (DEPTH, D_MODEL, D_IN), D_MODEL ** -0.5),
        "b_f": FORGET_BIAS_INIT + nrm(ks[13], (DEPTH, N_HEADS), 0.1),
        "conv_w": nrm(ks[14], (DEPTH, CONV_W, D_CONV), CONV_W ** -0.5),
        "w_out": nrm(ks[15], (DEPTH, D_MIX, D_MODEL), D_MIX ** -0.5),
        "g_mlp": 1.0 + nrm(ks[16], (DEPTH, D_MODEL), 0.02),
        "w_up": nrm(ks[17], (DEPTH, D_MODEL, D_FF), D_MODEL ** -0.5),
        "w_down": nrm(ks[18], (DEPTH, D_FF, D_MODEL), D_FF ** -0.5),
        "g_final": 1.0 + nrm(ks[19], (D_MODEL,), 0.02),
    }


def reference(x_prompt, x_sample, c_prompt, c_sample, cache_k, cache_v, cache_logf, state_conv, page_table,
              w_ada, b_ada, g_mix, w_in, b_f, conv_w, w_out, g_mlp, w_up, w_down, g_final):
    h_p, h_s = x_prompt, x_sample
    kp_l, vp_l, lfp_l, cp_l, ks_l, vs_l, lfs_l, cs_l = [], [], [], [], [], [], [], []
    for l in range(DEPTH):
        weights = (w_ada[l], b_ada[l], g_mix[l], w_in[l], b_f[l], conv_w[l], w_out[l], g_mlp[l], w_up[l], w_down[l])
        hist_p = jnp.zeros((h_p.shape[0], CONV_W - 1, D_CONV), h_p.dtype)
        h_p, kp, vp, lfp, cp = _layer(h_p, c_prompt, hist_p, _fox_prompt, *weights)
        attend_s = functools.partial(_fox_sample, cache_k=cache_k[l], cache_v=cache_v[l],
                                     cache_logf=cache_logf[l], page_table=page_table)
        h_s, ksn, vsn, lfs, cs = _layer(h_s, c_sample, state_conv[l], attend_s, *weights)
        kp_l.append(kp); vp_l.append(vp); lfp_l.append(lfp); cp_l.append(cp)
        ks_l.append(ksn); vs_l.append(vsn); lfs_l.append(lfs); cs_l.append(cs)
    y_prompt = _rmsnorm(h_p, g_final)
    y_sample = _rmsnorm(h_s, g_final)
    k_prompt = jnp.stack(kp_l)
    v_prompt = jnp.stack(vp_l)
    logf_prompt = jnp.stack(lfp_l)
    conv_prompt = jnp.stack(cp_l)
    k_sample = jnp.stack(ks_l)
    v_sample = jnp.stack(vs_l)
    logf_sample = jnp.stack(lfs_l)
    conv_sample = jnp.stack(cs_l)
    return (y_prompt, y_sample, k_prompt, v_prompt, logf_prompt, conv_prompt, k_sample, v_sample, logf_sample, conv_sample)
```

```python
import functools

import jax
import jax.numpy as jnp
from jax import lax
from jax.experimental import pallas as pl
from jax.experimental.pallas import tpu as pltpu

F32 = jnp.float32
BF16 = jnp.bfloat16
NORM_EPS = 1e-6
LANES = 128
MIB = 1024 * 1024
NEG_INF = float("-inf")


def _params(vmem_mib, semantics=None):
    return pltpu.CompilerParams(vmem_limit_bytes=vmem_mib * MIB, dimension_semantics=semantics)


def _dot(a, b):
    return jnp.dot(a, b, preferred_element_type=F32)


def _dot_nt(a, b):
    return lax.dot_general(a, b, (((1,), (1,)), ((), ())), preferred_element_type=F32)


def _rmsnorm(x, g):
    return (x * lax.rsqrt(jnp.mean(x * x, axis=-1, keepdims=True) + NORM_EPS)) * g


def _norm_mod(x, g, sc, sh):
    return _rmsnorm(x, g) * (1.0 + sc) + sh


def _resident(shape, index_map):
    return pl.BlockSpec(shape, index_map, pipeline_mode=pl.Buffered(1))


def _mod_spec(mod, tm):
    d = mod.shape[-1]
    if mod.shape[1] == 1:
        return pl.BlockSpec((1, 1, d), lambda b, t, *_: (b, 0, 0))
    return pl.BlockSpec((1, tm, d), lambda b, t, *_: (b, t, 0))


def _mod_kernel(c_ref, w_ref, b_ref, o_ref):
    c = c_ref[...]
    s = (c * jax.nn.sigmoid(c)).astype(BF16)
    o_ref[...] = _dot(s, w_ref[...].astype(BF16)) + b_ref[...]


def _modulation(c, w_ada, b_ada, tn=1024):
    n, d = c.shape
    dn = w_ada.shape[1]
    return pl.pallas_call(
        _mod_kernel,
        grid=(dn // tn,),
        in_specs=[
            pl.BlockSpec((n, d), lambda j: (0, 0)),
            pl.BlockSpec((d, tn), lambda j: (0, j)),
            pl.BlockSpec((1, tn), lambda j: (0, j)),
        ],
        out_specs=pl.BlockSpec((n, tn), lambda j: (0, j)),
        out_shape=jax.ShapeDtypeStruct((n, dn), F32),
        compiler_params=_params(40),
        name="modulation",
    )(c, w_ada, b_ada.reshape(1, dn))


def _conv_kernel(x_ref, g_ref, sc_ref, sh_ref, w_ref, cw_ref, a_ref, st_ref, carry_ref):
    t = pl.program_id(1)

    @pl.when(t == 0)
    def _():
        carry_ref[...] = jnp.zeros_like(carry_ref)

    u = _norm_mod(x_ref[0], g_ref[...], sc_ref[0], sh_ref[0]).astype(BF16)
    dc = cw_ref.shape[1]
    r = _dot(u, w_ref[...])
    cb, cc, cx = r[:, :dc], r[:, dc:2 * dc], r[:, 2 * dc:]
    z = cc * cx
    tm = z.shape[0]
    prev = carry_ref[...]
    row = lax.broadcasted_iota(jnp.int32, z.shape, 0)
    z1 = jnp.where(row >= 1, pltpu.roll(z, 1, 0), prev[7:8])
    z2 = jnp.where(row >= 2, pltpu.roll(z, 2, 0), jnp.where(row == 1, prev[7:8], prev[6:7]))
    cw = cw_ref[...]
    y = cw[0:1] * z2 + cw[1:2] * z1 + cw[2:3] * z
    a_ref[0] = (cb * y).astype(BF16)
    carry_ref[...] = z[tm - 8:tm]
    st_ref[0] = z[tm - 2:tm]


def _prompt_conv(x, g, sc, sh, w_in_bf, conv_w, tm=512):
    b, t, d = x.shape
    dc = conv_w.shape[1]
    return pl.pallas_call(
        _conv_kernel,
        grid=(b, t // tm),
        in_specs=[
            pl.BlockSpec((1, tm, d), lambda i, j: (i, j, 0)),
            _resident((1, d), lambda i, j: (0, 0)),
            _mod_spec(sc, tm),
            _mod_spec(sh, tm),
            _resident((d, 3 * dc), lambda i, j: (0, 0)),
            _resident((3, dc), lambda i, j: (0, 0)),
        ],
        out_specs=[
            pl.BlockSpec((1, tm, dc), lambda i, j: (i, j, 0)),
            pl.BlockSpec((1, 2, dc), lambda i, j: (i, 0, 0)),
        ],
        out_shape=[
            jax.ShapeDtypeStruct((b, t, dc), BF16),
            jax.ShapeDtypeStruct((b, 2, dc), F32),
        ],
        scratch_shapes=[pltpu.VMEM((8, dc), F32)],
        compiler_params=_params(48, ("arbitrary", "arbitrary")),
        name="prompt_conv",
    )(x, g, sc, sh, w_in_bf, conv_w)


def _qkv_kernel(x_ref, g_ref, sc_ref, sh_ref, w_ref, wf_ref, bf_ref,
                q_ref, k_ref, v_ref, lf_ref, lft_ref, *, scale):
    u = _norm_mod(x_ref[0], g_ref[...], sc_ref[0], sh_ref[0]).astype(BF16)
    da = q_ref.shape[-1]
    r = _dot(u, w_ref[...])
    q_ref[0] = (r[:, :da] * scale).astype(BF16)
    k_ref[0] = r[:, da:2 * da]
    v_ref[0] = r[:, 2 * da:]
    lf = jax.nn.log_sigmoid(_dot(u, wf_ref[...]) + bf_ref[...])
    lf_ref[0] = lf
    lft_ref[0] = lf.T


def _prompt_qkv(x, g, sc, sh, w_in_bf, wf_pad, bf_pad, da, scale, tm=512):
    b, t, d = x.shape
    return pl.pallas_call(
        functools.partial(_qkv_kernel, scale=scale),
        grid=(b, t // tm),
        in_specs=[
            pl.BlockSpec((1, tm, d), lambda i, j: (i, j, 0)),
            _resident((1, d), lambda i, j: (0, 0)),
            _mod_spec(sc, tm),
            _mod_spec(sh, tm),
            _resident((d, 3 * da), lambda i, j: (0, 1)),
            _resident((d, LANES), lambda i, j: (0, 0)),
            _resident((1, LANES), lambda i, j: (0, 0)),
        ],
        out_specs=[
            pl.BlockSpec((1, tm, da), lambda i, j: (i, j, 0)),
            pl.BlockSpec((1, tm, da), lambda i, j: (i, j, 0)),
            pl.BlockSpec((1, tm, da), lambda i, j: (i, j, 0)),
            pl.BlockSpec((1, tm, LANES), lambda i, j: (i, j, 0)),
            pl.BlockSpec((1, LANES, tm), lambda i, j: (i, 0, j)),
        ],
        out_shape=[
            jax.ShapeDtypeStruct((b, t, da), BF16),
            jax.ShapeDtypeStruct((b, t, da), F32),
            jax.ShapeDtypeStruct((b, t, da), F32),
            jax.ShapeDtypeStruct((b, t, LANES), F32),
            jax.ShapeDtypeStruct((b, LANES, t), F32),
        ],
        compiler_params=_params(52, ("arbitrary", "arbitrary")),
        name="prompt_qkv",
    )(x, g, sc, sh, w_in_bf, wf_pad, bf_pad)


def _cumsum_kernel(x_ref, o_ref):
    x = x_ref[0]
    t = x.shape[1]
    lane = lax.broadcasted_iota(jnp.int32, x.shape, 1)
    sh = 1
    while sh < t:
        x = x + jnp.where(lane >= sh, pltpu.roll(x, sh, 1), 0.0)
        sh *= 2
    o_ref[0] = x


def _prompt_cumsum(lft, n_heads):
    b, _, t = lft.shape
    return pl.pallas_call(
        _cumsum_kernel,
        grid=(b,),
        in_specs=[pl.BlockSpec((1, n_heads, t), lambda i: (i, 0, 0))],
        out_specs=pl.BlockSpec((1, n_heads, t), lambda i: (i, 0, 0)),
        out_shape=jax.ShapeDtypeStruct((b, n_heads, t), F32),
        name="prompt_cumsum",
    )(lft)


def _attn_kernel(q_ref, k_ref, v_ref, c_ref, o_ref, kb_ref, vb_ref, *, hd):
    hp = pl.program_id(1)
    qi = pl.program_id(2)

    @pl.when(qi == 0)
    def _():
        kb_ref[...] = k_ref[0].astype(BF16)
        vb_ref[...] = v_ref[0].astype(BF16)

    q = q_ref[0]
    tq = q.shape[0]
    lane = lax.broadcasted_iota(jnp.int32, q.shape, 1)
    row = lax.broadcasted_iota(jnp.int32, (tq, tq), 0)
    col = lax.broadcasted_iota(jnp.int32, (tq, tq), 1)
    outs = []
    for h2 in range(LANES // hd):
        in_head = (lane >= h2 * hd) & (lane < (h2 + 1) * hd)
        qm = jnp.where(in_head, q, jnp.zeros_like(q))
        h = hp * (LANES // hd) + h2

        def block(j, carry, masked):
            m, l, acc = carry
            off = pl.multiple_of(j * tq, tq)
            s = _dot_nt(qm, kb_ref[pl.ds(off, tq), :]) - c_ref[0, pl.ds(h, 1), pl.ds(off, tq)]
            if masked:
                s = jnp.where(row >= col, s, NEG_INF)
            m_new = jnp.maximum(m, jnp.max(s, axis=-1, keepdims=True))
            alpha = jnp.exp(m - m_new)
            p = jnp.exp(s - m_new)
            l = alpha * l + jnp.sum(p, axis=-1, keepdims=True)
            acc = alpha * acc + _dot(p.astype(BF16), vb_ref[pl.ds(off, tq), :])
            return m_new, l, acc

        init = (jnp.full((tq, 1), NEG_INF, F32), jnp.zeros((tq, 1), F32), jnp.zeros((tq, LANES), F32))
        carry = lax.fori_loop(0, qi, lambda j, c: block(j, c, False), init)
        _, l, acc = block(qi, carry, True)
        outs.append(acc / l)
    o = outs[0]
    for h2 in range(1, len(outs)):
        o = jnp.where(lane >= h2 * hd, outs[h2], o)
    o_ref[0] = o.astype(BF16)


def _prompt_attn(q, k, v, crow, hd, tq=512):
    b, t, da = q.shape
    n_heads = crow.shape[1]
    return pl.pallas_call(
        functools.partial(_attn_kernel, hd=hd),
        grid=(b, da // LANES, t // tq),
        in_specs=[
            pl.BlockSpec((1, tq, LANES), lambda n, h, i: (n, i, h)),
            pl.BlockSpec((1, t, LANES), lambda n, h, i: (n, 0, h)),
            pl.BlockSpec((1, t, LANES), lambda n, h, i: (n, 0, h)),
            pl.BlockSpec((1, n_heads, t), lambda n, h, i: (n, 0, 0)),
        ],
        out_specs=pl.BlockSpec((1, tq, LANES), lambda n, h, i: (n, i, h)),
        out_shape=jax.ShapeDtypeStruct((b, t, da), BF16),
        scratch_shapes=[pltpu.VMEM((t, LANES), BF16), pltpu.VMEM((t, LANES), BF16)],
        compiler_params=_params(32, ("arbitrary", "arbitrary", "arbitrary")),
        name="prompt_attn",
    )(q, k, v, crow)


def _mix_kernel(x_ref, a_ref, b_ref, g1_ref, sca_ref, sha_ref, gta_ref, g2_ref, scm_ref, shm_ref,
                wg_ref, wo_ref, h_ref, m_ref, *, tc):
    x = x_ref[0]
    d = x.shape[1]
    dc = a_ref.shape[-1]
    u = _norm_mod(x, g1_ref[...], sca_ref[0], sha_ref[0]).astype(BF16)
    a = a_ref[0]
    b = b_ref[0]
    gta = gta_ref[0]
    for c in range(d // tc):
        lo, hi = c * tc, (c + 1) * tc
        ga = _dot(u, wg_ref[:, lo:hi])
        gb = _dot(u, wg_ref[:, d + lo:d + hi])
        pa = _dot(a, wo_ref[:dc, lo:hi])
        pb = _dot(b, wo_ref[dc:, lo:hi])
        mixed = jax.nn.sigmoid(ga) * pa + jax.nn.sigmoid(gb) * pb
        h_ref[0, :, lo:hi] = x[:, lo:hi] + gta[:, lo:hi] * mixed
    m_ref[0] = _norm_mod(h_ref[0], g2_ref[...], scm_ref[0], shm_ref[0]).astype(BF16)


def _mix(x, a, bb, g1, sca, sha, gta, g2, scm, shm, w_gate_bf, w_out_bf, tm, tc=512):
    b, t, d = x.shape
    dc = a.shape[-1]
    da = bb.shape[-1]
    row = lambda i, j: (i, j, 0)
    const = lambda i, j: (0, 0)
    return pl.pallas_call(
        functools.partial(_mix_kernel, tc=tc),
        grid=(b, t // tm),
        in_specs=[
            pl.BlockSpec((1, tm, d), row),
            pl.BlockSpec((1, tm, dc), row),
            pl.BlockSpec((1, tm, da), row),
            _resident((1, d), const),
            _mod_spec(sca, tm), _mod_spec(sha, tm), _mod_spec(gta, tm),
            _resident((1, d), const),
            _mod_spec(scm, tm), _mod_spec(shm, tm),
            _resident((d, 2 * d), const),
            _resident((dc + da, d), const),
        ],
        out_specs=[pl.BlockSpec((1, tm, d), row), pl.BlockSpec((1, tm, d), row)],
        out_shape=[jax.ShapeDtypeStruct((b, t, d), F32), jax.ShapeDtypeStruct((b, t, d), BF16)],
        compiler_params=_params(56, ("arbitrary", "arbitrary")),
        name="mix",
    )(x, a, bb, g1, sca, sha, gta, g2, scm, shm, w_gate_bf, w_out_bf)


def _mlp_kernel(m_ref, h_ref, gt_ref, gf_ref, wu_ref, wd_ref, y_ref, acc_ref):
    f = pl.program_id(2)

    @pl.when(f == 0)
    def _():
        acc_ref[...] = jnp.zeros_like(acc_ref)

    hid = _dot(m_ref[0], wu_ref[...])
    hid = jnp.square(jnp.maximum(hid, 0.0)).astype(BF16)
    acc_ref[...] += _dot(hid, wd_ref[...])

    @pl.when(f == pl.num_programs(2) - 1)
    def _():
        h2 = h_ref[0] + gt_ref[0] * acc_ref[...]
        y_ref[0] = _rmsnorm(h2, gf_ref[...])


def _mlp(m, h, gtm, g_final, w_up_bf, w_down_bf, tm, tf=1024):
    b, t, d = h.shape
    dff = w_up_bf.shape[1]
    row = lambda i, j, f: (i, j, 0)
    return pl.pallas_call(
        _mlp_kernel,
        grid=(b, t // tm, dff // tf),
        in_specs=[
            pl.BlockSpec((1, tm, d), row),
            pl.BlockSpec((1, tm, d), row),
            _mod_spec(gtm, tm),
            _resident((1, d), lambda i, j, f: (0, 0)),
            pl.BlockSpec((d, tf), lambda i, j, f: (0, f)),
            pl.BlockSpec((tf, d), lambda i, j, f: (f, 0)),
        ],
        out_specs=pl.BlockSpec((1, tm, d), row),
        out_shape=jax.ShapeDtypeStruct((b, t, d), F32),
        scratch_shapes=[pltpu.VMEM((tm, d), F32)],
        compiler_params=_params(52, ("arbitrary", "arbitrary", "arbitrary")),
        name="mlp",
    )(m, h, gtm, g_final, w_up_bf, w_down_bf)


def _sample_proj_kernel(x_ref, g_ref, sc_ref, sh_ref, wb_ref, wc_ref, wx_ref, wq_ref, wk_ref, wv_ref,
                        wf_ref, bf_ref, h0_ref, h1_ref, cw_ref,
                        a_ref, z_ref, q_ref, k_ref, v_ref, lf_ref, *, scale):
    u = _norm_mod(x_ref[...], g_ref[...], sc_ref[...], sh_ref[...]).astype(BF16)
    cb = _dot(u, wb_ref[...])
    z = _dot(u, wc_ref[...]) * _dot(u, wx_ref[...])
    cw = cw_ref[...]
    y = cw[0:1] * h0_ref[...] + cw[1:2] * h1_ref[...] + cw[2:3] * z
    a_ref[...] = (cb * y).astype(BF16)
    z_ref[...] = z
    q_ref[...] = _dot(u, wq_ref[...]) * scale
    k_ref[...] = _dot(u, wk_ref[...])
    v_ref[...] = _dot(u, wv_ref[...])

    @pl.when(pl.program_id(0) == 0)
    def _():
        lf_ref[...] = jax.nn.log_sigmoid(_dot(u, wf_ref[...]) + bf_ref[...])


def _sample_proj(x, g, sc, sh, w_in_bf, wf_pad, bf_pad, hist0, hist1, conv_w, scale, tc=512):
    n, d = x.shape
    dc = conv_w.shape[1]
    nc = dc // tc
    whole = lambda shape: pl.BlockSpec(shape, lambda c: (0, 0))
    cols = pl.BlockSpec((n, tc), lambda c: (0, c))
    wspec = lambda r: pl.BlockSpec((d, tc), lambda c, r=r: (0, r * nc + c))
    return pl.pallas_call(
        functools.partial(_sample_proj_kernel, scale=scale),
        grid=(nc,),
        in_specs=[whole((n, d)), whole((1, d)), whole((n, d)), whole((n, d))]
        + [wspec(r) for r in range(6)]
        + [whole((d, LANES)), whole((1, LANES)), cols, cols, pl.BlockSpec((3, tc), lambda c: (0, c))],
        out_specs=[cols, cols, cols, cols, cols, whole((n, LANES))],
        out_shape=[jax.ShapeDtypeStruct((n, dc), BF16)]
        + [jax.ShapeDtypeStruct((n, dc), F32)] * 4
        + [jax.ShapeDtypeStruct((n, LANES), F32)],
        compiler_params=_params(40, ("arbitrary",)),
        name="sample_proj",
    )(x, g, sc, sh, *([w_in_bf] * 6), wf_pad, bf_pad, hist0, hist1, conv_w)


def _sample_attn_kernel(pt_ref, q_ref, kn_ref, vn_ref, lfn_ref, *rest, pages, n_heads, hd):
    k_refs = rest[:pages]
    v_refs = rest[pages:2 * pages]
    lf_refs = rest[2 * pages:3 * pages]
    o_ref, m_ref, l_ref, acc_ref, carry_ref = rest[3 * pages:]
    g = pl.program_id(1)
    da = q_ref.shape[-1]
    psz = k_refs[0].shape[1]

    hrow = lax.broadcasted_iota(jnp.int32, (n_heads, da), 0)
    ccol = lax.broadcasted_iota(jnp.int32, (n_heads, da), 1)
    own = (ccol >= hrow * hd) & (ccol < (hrow + 1) * hd)
    qmat = jnp.where(own, q_ref[0], 0.0)
    qmat_bf = qmat.astype(BF16)
    cn = lfn_ref[0]

    @pl.when(g == 0)
    def _():
        s_new = jnp.sum(qmat * kn_ref[0], axis=1, keepdims=True)
        m_ref[...] = s_new
        l_ref[...] = jnp.ones_like(l_ref)
        acc_ref[...] = jnp.broadcast_to(vn_ref[0], acc_ref.shape)
        carry_ref[...] = jnp.zeros_like(carry_ref)

    lane = lax.broadcasted_iota(jnp.int32, (n_heads, psz), 1)
    carry = carry_ref[...]
    logits = [None] * pages
    for r in reversed(range(pages)):
        s = _dot_nt(qmat_bf, k_refs[r][0].astype(BF16))
        lft = lf_refs[r][0].T
        suf = jnp.where(lane < psz - 1, pltpu.roll(lft, psz - 1, 1), 0.0)
        sh = 1
        while sh < psz:
            suf = suf + jnp.where(lane < psz - sh, pltpu.roll(suf, psz - sh, 1), 0.0)
            sh *= 2
        logits[r] = s + (cn + carry) + suf
        carry = carry + suf[:, 0:1] + lft[:, 0:1]
    carry_ref[...] = carry
    lg = jnp.concatenate(logits, axis=1)
    m_old = m_ref[...]
    m_new = jnp.maximum(m_old, jnp.max(lg, axis=1, keepdims=True))
    alpha = jnp.exp(m_old - m_new)
    p = jnp.exp(lg - m_new)
    l_ref[...] = alpha * l_ref[...] + jnp.sum(p, axis=1, keepdims=True)
    m_ref[...] = m_new
    v_all = jnp.concatenate([v_refs[r][0].astype(BF16) for r in range(pages)], axis=0)
    acc_ref[...] = alpha * acc_ref[...] + _dot(p.astype(BF16), v_all)

    @pl.when(g == pl.num_programs(1) - 1)
    def _():
        o = jnp.where(own, acc_ref[...] / l_ref[...], 0.0)
        o_ref[0] = jnp.sum(o, axis=0, keepdims=True).astype(o_ref.dtype)


def _sample_attn(q, k_new, v_new, lf_new_t, cache_k, cache_v, cache_lf, page_table, n_heads, hd, pages=8):
    nb, _, da = q.shape
    n_pages = page_table.shape[1]
    psz = cache_k.shape[1]
    ng = n_pages // pages

    def page_map(r):
        return lambda b, g, pt: (pt[b, (ng - 1 - g) * pages + r], 0, 0)

    tok = pl.BlockSpec((1, 1, da), lambda b, g, pt: (b, 0, 0))
    grid_spec = pltpu.PrefetchScalarGridSpec(
        num_scalar_prefetch=1,
        grid=(nb, ng),
        in_specs=[tok, tok, tok, pl.BlockSpec((1, n_heads, 1), lambda b, g, pt: (b, 0, 0))]
        + [pl.BlockSpec((1, psz, da), page_map(r)) for r in range(pages)]
        + [pl.BlockSpec((1, psz, da), page_map(r)) for r in range(pages)]
        + [pl.BlockSpec((1, psz, n_heads), page_map(r)) for r in range(pages)],
        out_specs=tok,
        scratch_shapes=[
            pltpu.VMEM((n_heads, 1), F32),
            pltpu.VMEM((n_heads, 1), F32),
            pltpu.VMEM((n_heads, da), F32),
            pltpu.VMEM((n_heads, 1), F32),
        ],
    )
    return pl.pallas_call(
        functools.partial(_sample_attn_kernel, pages=pages, n_heads=n_heads, hd=hd),
        grid_spec=grid_spec,
        out_shape=jax.ShapeDtypeStruct((nb, 1, da), BF16),
        compiler_params=_params(40, ("arbitrary", "arbitrary")),
        name="sample_attn",
    )(page_table, q, k_new, v_new, lf_new_t, *([cache_k] * pages), *([cache_v] * pages), *([cache_lf] * pages))


def kernel(x_prompt, x_sample, c_prompt, c_sample, cache_k, cache_v, cache_logf, state_conv, page_table,
           w_ada, b_ada, g_mix, w_in, b_f, conv_w, w_out, g_mlp, w_up, w_down, g_final):
    depth = w_ada.shape[0]
    assert depth == 1, "single-layer trunk"
    nb, seq, d = x_prompt.shape
    ns, dec_seq, _ = x_sample.shape
    assert dec_seq == 1, "one new token per sampled sequence"
    n_heads = b_f.shape[-1]
    hd = cache_k.shape[-1]
    da = n_heads * hd
    dc = conv_w.shape[-1]
    n_pool, psz = cache_k.shape[1], cache_k.shape[2]
    scale = hd ** -0.5
    assert LANES % hd == 0 and w_in.shape[-1] == 3 * dc + 3 * da + n_heads + 2 * d

    w_in_l = w_in[0]
    w_in_bf = w_in_l[:, :3 * dc + 3 * da].astype(BF16)
    wf_pad = jnp.pad(w_in_l[:, 3 * dc + 3 * da:3 * dc + 3 * da + n_heads], ((0, 0), (0, LANES - n_heads))).astype(BF16)
    bf_pad = jnp.pad(b_f[0], (0, LANES - n_heads)).reshape(1, LANES)
    w_gate_bf = w_in_l[:, 3 * dc + 3 * da + n_heads:].astype(BF16)
    w_out_bf = w_out[0].astype(BF16)
    w_up_bf = w_up[0].astype(BF16)
    w_down_bf = w_down[0].astype(BF16)
    g1 = g_mix[0].reshape(1, d)
    g2 = g_mlp[0].reshape(1, d)
    gf = g_final.reshape(1, d)
    cw = conv_w[0]

    mod = _modulation(jnp.concatenate([c_prompt, c_sample], axis=0), w_ada[0], b_ada[0])
    mod_p = [m.reshape(nb, 1, d) for m in jnp.split(mod[:nb], 6, axis=-1)]
    mod_s = [m.reshape(1, ns, d) for m in jnp.split(mod[nb:], 6, axis=-1)]

    sh_a, sc_a, gt_a, sh_m, sc_m, gt_m = mod_p
    a_p, conv_p = _prompt_conv(x_prompt, g1, sc_a, sh_a, w_in_bf, cw)
    q_p, k_p, v_p, lf_p, lft_p = _prompt_qkv(x_prompt, g1, sc_a, sh_a, w_in_bf, wf_pad, bf_pad, da, scale)
    crow = _prompt_cumsum(lft_p, n_heads)
    b_p = _prompt_attn(q_p, k_p, v_p, crow, hd)
    h_p, m_p = _mix(x_prompt, a_p, b_p, g1, sc_a, sh_a, gt_a, g2, sc_m, sh_m, w_gate_bf, w_out_bf, tm=256)
    y_prompt = _mlp(m_p, h_p, gt_m, gf, w_up_bf, w_down_bf, tm=512)

    sh_a, sc_a, gt_a, sh_m, sc_m, gt_m = mod_s
    xs = x_sample.reshape(ns, d)
    hist = state_conv[0]
    a_s, z_s, q_s, k_s, v_s, lf_s = _sample_proj(
        xs, g1, sc_a[0], sh_a[0], w_in_bf, wf_pad, bf_pad, hist[:, 0], hist[:, 1], cw, scale)
    lf_s = lf_s[:, :n_heads]
    b_s = _sample_attn(
        q_s.reshape(ns, 1, da), k_s.reshape(ns, 1, da), v_s.reshape(ns, 1, da), lf_s.reshape(ns, n_heads, 1),
        cache_k[0].reshape(n_pool, psz, da), cache_v[0].reshape(n_pool, psz, da), cache_logf[0],
        page_table, n_heads, hd)
    xs3 = xs.reshape(1, ns, d)
    h_s, m_s = _mix(xs3, a_s.reshape(1, ns, dc), b_s.reshape(1, ns, da), g1, sc_a, sh_a, gt_a, g2, sc_m, sh_m,
                    w_gate_bf, w_out_bf, tm=ns)
    y_sample = _mlp(m_s, h_s, gt_m, gf, w_up_bf, w_down_bf, tm=ns).reshape(ns, 1, d)

    return (
        y_prompt,
        y_sample,
        k_p.reshape(1, nb, seq, n_heads, hd),
        v_p.reshape(1, nb, seq, n_heads, hd),
        lf_p[:, :, :n_heads].reshape(1, nb, seq, n_heads),
        conv_p.reshape(1, nb, 2, dc),
        k_s.reshape(1, ns, 1, n_heads, hd),
        v_s.reshape(1, ns, 1, n_heads, hd),
        lf_s.reshape(1, ns, 1, n_heads),
        jnp.stack([hist[:, 1], z_s], axis=1).reshape(1, ns, 2, dc),
    )
```

```python
import functools

import jax
import jax.numpy as jnp
from jax import lax
from jax.experimental import pallas as pl
from jax.experimental.pallas import tpu as pltpu

F32 = jnp.float32
BF16 = jnp.bfloat16
NORM_EPS = 1e-6
LANES = 128
MIB = 1024 * 1024
NEG_INF = float("-inf")


def _params(vmem_mib, semantics=None):
    return pltpu.CompilerParams(vmem_limit_bytes=vmem_mib * MIB, dimension_semantics=semantics)


def _dot(a, b):
    return jnp.dot(a, b, preferred_element_type=F32)


def _dot_nt(a, b):
    return lax.dot_general(a, b, (((1,), (1,)), ((), ())), preferred_element_type=F32)


def _rmsnorm(x, g):
    return (x * lax.rsqrt(jnp.mean(x * x, axis=-1, keepdims=True) + NORM_EPS)) * g


def _norm_mod(x, g, sc, sh):
    return _rmsnorm(x, g) * (1.0 + sc) + sh


def _resident(shape, index_map):
    return pl.BlockSpec(shape, index_map, pipeline_mode=pl.Buffered(1))


def _mod_spec(mod, tm):
    d = mod.shape[-1]
    if mod.shape[1] == 1:
        return pl.BlockSpec((1, 1, d), lambda b, t, *_: (b, 0, 0))
    return pl.BlockSpec((1, tm, d), lambda b, t, *_: (b, t, 0))


def _mod_kernel(c_ref, w_ref, b_ref, o_ref):
    c = c_ref[...]
    s = (c * jax.nn.sigmoid(c)).astype(BF16)
    o_ref[...] = _dot(s, w_ref[...].astype(BF16)) + b_ref[...]


def _modulation(c, w_ada, b_ada, tn=1024):
    n, d = c.shape
    dn = w_ada.shape[1]
    return pl.pallas_call(
        _mod_kernel,
        grid=(dn // tn,),
        in_specs=[
            pl.BlockSpec((n, d), lambda j: (0, 0)),
            pl.BlockSpec((d, tn), lambda j: (0, j)),
            pl.BlockSpec((1, tn), lambda j: (0, j)),
        ],
        out_specs=pl.BlockSpec((n, tn), lambda j: (0, j)),
        out_shape=jax.ShapeDtypeStruct((n, dn), F32),
        compiler_params=_params(40),
        name="modulation",
    )(c, w_ada, b_ada.reshape(1, dn))


def _conv_kernel(x_ref, g_ref, sc_ref, sh_ref, w_ref, cw_ref, a_ref, st_ref, carry_ref):
    t = pl.program_id(1)

    @pl.when(t == 0)
    def _():
        carry_ref[...] = jnp.zeros_like(carry_ref)

    u = _norm_mod(x_ref[0], g_ref[...], sc_ref[0], sh_ref[0]).astype(BF16)
    dc = cw_ref.shape[1]
    r = _dot_nt(u, w_ref[...])
    cb, cc, cx = r[:, :dc], r[:, dc:2 * dc], r[:, 2 * dc:]
    z = cc * cx
    tm = z.shape[0]
    prev = carry_ref[...]
    row = lax.broadcasted_iota(jnp.int32, z.shape, 0)
    z1 = jnp.where(row >= 1, pltpu.roll(z, 1, 0), prev[7:8])
    z2 = jnp.where(row >= 2, pltpu.roll(z, 2, 0), jnp.where(row == 1, prev[7:8], prev[6:7]))
    cw = cw_ref[...]
    y = cw[0:1] * z2 + cw[1:2] * z1 + cw[2:3] * z
    a_ref[0] = (cb * y).astype(BF16)
    carry_ref[...] = z[tm - 8:tm]
    st_ref[0] = z[tm - 2:tm]


def _prompt_conv(x, g, sc, sh, wt_conv, conv_w, tm=512):
    b, t, d = x.shape
    dc = conv_w.shape[1]
    return pl.pallas_call(
        _conv_kernel,
        grid=(b, t // tm),
        in_specs=[
            pl.BlockSpec((1, tm, d), lambda i, j: (i, j, 0)),
            _resident((1, d), lambda i, j: (0, 0)),
            _mod_spec(sc, tm),
            _mod_spec(sh, tm),
            _resident((3 * dc, d), lambda i, j: (0, 0)),
            _resident((3, dc), lambda i, j: (0, 0)),
        ],
        out_specs=[
            pl.BlockSpec((1, tm, dc), lambda i, j: (i, j, 0)),
            pl.BlockSpec((1, 2, dc), lambda i, j: (i, 0, 0)),
        ],
        out_shape=[
            jax.ShapeDtypeStruct((b, t, dc), BF16),
            jax.ShapeDtypeStruct((b, 2, dc), F32),
        ],
        scratch_shapes=[pltpu.VMEM((8, dc), F32)],
        compiler_params=_params(56, ("arbitrary", "arbitrary")),
        name="prompt_conv",
    )(x, g, sc, sh, wt_conv, conv_w)


def _qkv_kernel(x_ref, g_ref, sc_ref, sh_ref, w_ref, bf_ref, qt_ref, kt_ref, vt_ref, lft_ref, *, scale):
    u = _norm_mod(x_ref[0], g_ref[...], sc_ref[0], sh_ref[0]).astype(BF16)
    da = qt_ref.shape[1]
    r = _dot_nt(w_ref[...], u)
    qt_ref[0] = (r[:da] * scale).astype(BF16)
    kt_ref[0] = r[da:2 * da]
    vt_ref[0] = r[2 * da:3 * da]
    lft_ref[0] = jax.nn.log_sigmoid(r[3 * da:] + bf_ref[...])


def _prompt_qkv(x, g, sc, sh, wt_qkvf, bf_col, da, scale, tm=512):
    b, t, d = x.shape
    n_heads = bf_col.shape[0]
    feat = lambda i, j: (i, 0, j)
    return pl.pallas_call(
        functools.partial(_qkv_kernel, scale=scale),
        grid=(b, t // tm),
        in_specs=[
            pl.BlockSpec((1, tm, d), lambda i, j: (i, j, 0)),
            _resident((1, d), lambda i, j: (0, 0)),
            _mod_spec(sc, tm),
            _mod_spec(sh, tm),
            _resident((3 * da + n_heads, d), lambda i, j: (0, 0)),
            _resident((n_heads, 1), lambda i, j: (0, 0)),
        ],
        out_specs=[
            pl.BlockSpec((1, da, tm), feat),
            pl.BlockSpec((1, da, tm), feat),
            pl.BlockSpec((1, da, tm), feat),
            pl.BlockSpec((1, n_heads, tm), feat),
        ],
        out_shape=[
            jax.ShapeDtypeStruct((b, da, t), BF16),
            jax.ShapeDtypeStruct((b, da, t), F32),
            jax.ShapeDtypeStruct((b, da, t), F32),
            jax.ShapeDtypeStruct((b, n_heads, t), F32),
        ],
        compiler_params=_params(56, ("arbitrary", "arbitrary")),
        name="prompt_qkv",
    )(x, g, sc, sh, wt_qkvf, bf_col)


def _cumsum_kernel(x_ref, o_ref):
    x = x_ref[0]
    t = x.shape[1]
    lane = lax.broadcasted_iota(jnp.int32, x.shape, 1)
    sh = 1
    while sh < t:
        x = x + jnp.where(lane >= sh, pltpu.roll(x, sh, 1), 0.0)
        sh *= 2
    o_ref[0] = x


def _prompt_cumsum(lft):
    b, n_heads, t = lft.shape
    return pl.pallas_call(
        _cumsum_kernel,
        grid=(b,),
        in_specs=[pl.BlockSpec((1, n_heads, t), lambda i: (i, 0, 0))],
        out_specs=pl.BlockSpec((1, n_heads, t), lambda i: (i, 0, 0)),
        out_shape=jax.ShapeDtypeStruct((b, n_heads, t), F32),
        name="prompt_cumsum",
    )(lft)


def _attn_kernel(qt_ref, kt_ref, vt_ref, c_ref, o_ref, k_sc, vt_sc, c_sc, *, hd):
    qi = pl.program_id(2)
    nh = LANES // hd

    @pl.when(qi == 0)
    def _():
        k_sc[...] = kt_ref[0].T.astype(BF16)
        vt_sc[...] = vt_ref[0].astype(BF16)
        c_sc[...] = c_ref[0, 0].T

    qt = qt_ref[0]
    tq = qt.shape[1]
    frow = lax.broadcasted_iota(jnp.int32, qt.shape, 0)
    qts = [jnp.where((frow >= h * hd) & (frow < (h + 1) * hd), qt, jnp.zeros_like(qt)) for h in range(nh)]
    srow = lax.broadcasted_iota(jnp.int32, (tq, tq), 0)
    tcol = lax.broadcasted_iota(jnp.int32, (tq, tq), 1)

    def block(j, carry, masked):
        off = pl.multiple_of(j * tq, tq)
        kb = k_sc[pl.ds(off, tq), :]
        vb = vt_sc[:, pl.ds(off, tq)]
        cb = c_sc[pl.ds(off, tq), :]
        new = []
        for h in range(nh):
            m, l, acc = carry[h]
            s = _dot(kb, qts[h]) - cb[:, h:h + 1]
            if masked:
                s = jnp.where(srow <= tcol, s, NEG_INF)
            m_new = jnp.maximum(m, jnp.max(s, axis=0, keepdims=True))
            alpha = jnp.exp(m - m_new)
            p = jnp.exp(s - m_new)
            l = alpha * l + jnp.sum(p, axis=0, keepdims=True)
            acc = alpha * acc + _dot(vb, p.astype(BF16))
            new.append((m_new, l, acc))
        return tuple(new)

    init = tuple((jnp.full((1, tq), NEG_INF, F32), jnp.zeros((1, tq), F32), jnp.zeros((LANES, tq), F32))
                 for _ in range(nh))
    carry = lax.fori_loop(0, qi, lambda j, c: block(j, c, False), init)
    carry = block(qi, carry, True)
    ot = carry[0][2] / carry[0][1]
    for h in range(1, nh):
        ot = jnp.where(frow >= h * hd, carry[h][2] / carry[h][1], ot)
    o_ref[0] = ot.T.astype(BF16)


def _prompt_attn(qt, kt, vt, crow, hd, tq=512):
    b, da, t = qt.shape
    nh = LANES // hd
    groups = da // LANES
    cg = crow.reshape(b, groups, nh, t)
    return pl.pallas_call(
        functools.partial(_attn_kernel, hd=hd),
        grid=(b, groups, t // tq),
        in_specs=[
            pl.BlockSpec((1, LANES, tq), lambda n, h, i: (n, h, i)),
            pl.BlockSpec((1, LANES, t), lambda n, h, i: (n, h, 0)),
            pl.BlockSpec((1, LANES, t), lambda n, h, i: (n, h, 0)),
            pl.BlockSpec((1, 1, nh, t), lambda n, h, i: (n, h, 0, 0)),
        ],
        out_specs=pl.BlockSpec((1, tq, LANES), lambda n, h, i: (n, i, h)),
        out_shape=jax.ShapeDtypeStruct((b, t, da), BF16),
        scratch_shapes=[
            pltpu.VMEM((t, LANES), BF16),
            pltpu.VMEM((LANES, t), BF16),
            pltpu.VMEM((t, nh), F32),
        ],
        compiler_params=_params(40, ("arbitrary", "arbitrary", "arbitrary")),
        name="prompt_attn",
    )(qt, kt, vt, cg)


def _mix_kernel(x_ref, a_ref, b_ref, g1_ref, sca_ref, sha_ref, gta_ref, g2_ref, scm_ref, shm_ref,
                wg_ref, wo_ref, h_ref, m_ref, *, tc):
    x = x_ref[0]
    d = x.shape[1]
    dc = a_ref.shape[-1]
    u = _norm_mod(x, g1_ref[...], sca_ref[0], sha_ref[0]).astype(BF16)
    a = a_ref[0]
    b = b_ref[0]
    gta = gta_ref[0]
    for c in range(d // tc):
        lo, hi = c * tc, (c + 1) * tc
        ga = _dot_nt(u, wg_ref[lo:hi, :])
        gb = _dot_nt(u, wg_ref[d + lo:d + hi, :])
        pa = _dot(a, wo_ref[:dc, lo:hi])
        pb = _dot(b, wo_ref[dc:, lo:hi])
        mixed = jax.nn.sigmoid(ga) * pa + jax.nn.sigmoid(gb) * pb
        h_ref[0, :, lo:hi] = x[:, lo:hi] + gta[:, lo:hi] * mixed
    m_ref[0] = _norm_mod(h_ref[0], g2_ref[...], scm_ref[0], shm_ref[0]).astype(BF16)


def _mix(x, a, bb, g1, sca, sha, gta, g2, scm, shm, wt_gate, w_out_bf, tm, tc=512):
    b, t, d = x.shape
    dc = a.shape[-1]
    da = bb.shape[-1]
    row = lambda i, j: (i, j, 0)
    const = lambda i, j: (0, 0)
    return pl.pallas_call(
        functools.partial(_mix_kernel, tc=tc),
        grid=(b, t // tm),
        in_specs=[
            pl.BlockSpec((1, tm, d), row),
            pl.BlockSpec((1, tm, dc), row),
            pl.BlockSpec((1, tm, da), row),
            _resident((1, d), const),
            _mod_spec(sca, tm), _mod_spec(sha, tm), _mod_spec(gta, tm),
            _resident((1, d), const),
            _mod_spec(scm, tm), _mod_spec(shm, tm),
            _resident((2 * d, d), const),
            _resident((dc + da, d), const),
        ],
        out_specs=[pl.BlockSpec((1, tm, d), row), pl.BlockSpec((1, tm, d), row)],
        out_shape=[jax.ShapeDtypeStruct((b, t, d), F32), jax.ShapeDtypeStruct((b, t, d), BF16)],
        compiler_params=_params(56, ("arbitrary", "arbitrary")),
        name="mix",
    )(x, a, bb, g1, sca, sha, gta, g2, scm, shm, wt_gate, w_out_bf)


def _mlp_kernel(m_ref, h_ref, gt_ref, gf_ref, wu_ref, wd_ref, y_ref, acc_ref):
    f = pl.program_id(2)

    @pl.when(f == 0)
    def _():
        acc_ref[...] = jnp.zeros_like(acc_ref)

    hid = _dot(m_ref[0], wu_ref[...])
    hid = jnp.square(jnp.maximum(hid, 0.0)).astype(BF16)
    acc_ref[...] += _dot(hid, wd_ref[...])

    @pl.when(f == pl.num_programs(2) - 1)
    def _():
        h2 = h_ref[0] + gt_ref[0] * acc_ref[...]
        y_ref[0] = _rmsnorm(h2, gf_ref[...])


def _mlp(m, h, gtm, g_final, w_up_bf, w_down_bf, tm, tf=1024):
    b, t, d = h.shape
    dff = w_up_bf.shape[1]
    row = lambda i, j, f: (i, j, 0)
    return pl.pallas_call(
        _mlp_kernel,
        grid=(b, t // tm, dff // tf),
        in_specs=[
            pl.BlockSpec((1, tm, d), row),
            pl.BlockSpec((1, tm, d), row),
            _mod_spec(gtm, tm),
            _resident((1, d), lambda i, j, f: (0, 0)),
            pl.BlockSpec((d, tf), lambda i, j, f: (0, f)),
            pl.BlockSpec((tf, d), lambda i, j, f: (f, 0)),
        ],
        out_specs=pl.BlockSpec((1, tm, d), row),
        out_shape=jax.ShapeDtypeStruct((b, t, d), F32),
        scratch_shapes=[pltpu.VMEM((tm, d), F32)],
        compiler_params=_params(52, ("arbitrary", "arbitrary", "arbitrary")),
        name="mlp",
    )(m, h, gtm, g_final, w_up_bf, w_down_bf)


def _sample_proj_kernel(x_ref, g_ref, sc_ref, sh_ref, wb_ref, wc_ref, wx_ref, wq_ref, wk_ref, wv_ref,
                        wf_ref, bf_ref, h0_ref, h1_ref, cw_ref,
                        a_ref, z_ref, q_ref, k_ref, v_ref, lf_ref, *, scale):
    u = _norm_mod(x_ref[...], g_ref[...], sc_ref[...], sh_ref[...]).astype(BF16)
    cb = _dot_nt(u, wb_ref[...])
    z = _dot_nt(u, wc_ref[...]) * _dot_nt(u, wx_ref[...])
    cw = cw_ref[...]
    y = cw[0:1] * h0_ref[...] + cw[1:2] * h1_ref[...] + cw[2:3] * z
    a_ref[...] = (cb * y).astype(BF16)
    z_ref[...] = z
    q_ref[...] = _dot_nt(u, wq_ref[...]) * scale
    k_ref[...] = _dot_nt(u, wk_ref[...])
    v_ref[...] = _dot_nt(u, wv_ref[...])

    @pl.when(pl.program_id(0) == 0)
    def _():
        lf_ref[...] = jax.nn.log_sigmoid(_dot_nt(u, wf_ref[...]) + bf_ref[...])


def _sample_proj(x, g, sc, sh, wt_cqkv, wt_f, bf_row, hist0, hist1, conv_w, scale, tc=512):
    n, d = x.shape
    dc = conv_w.shape[1]
    n_heads = wt_f.shape[0]
    nc = dc // tc
    whole = lambda shape: pl.BlockSpec(shape, lambda c: (0, 0))
    cols = pl.BlockSpec((n, tc), lambda c: (0, c))
    wspec = lambda r: pl.BlockSpec((tc, d), lambda c, r=r: (r * nc + c, 0))
    return pl.pallas_call(
        functools.partial(_sample_proj_kernel, scale=scale),
        grid=(nc,),
        in_specs=[whole((n, d)), whole((1, d)), whole((n, d)), whole((n, d))]
        + [wspec(r) for r in range(6)]
        + [whole((n_heads, d)), whole((1, n_heads)), cols, cols, pl.BlockSpec((3, tc), lambda c: (0, c))],
        out_specs=[cols, cols, cols, cols, cols, whole((n, n_heads))],
        out_shape=[jax.ShapeDtypeStruct((n, dc), BF16)]
        + [jax.ShapeDtypeStruct((n, dc), F32)] * 4
        + [jax.ShapeDtypeStruct((n, n_heads), F32)],
        compiler_params=_params(40, ("arbitrary",)),
        name="sample_proj",
    )(x, g, sc, sh, *([wt_cqkv] * 6), wt_f, bf_row, hist0, hist1, conv_w)


def _sample_attn_kernel(pt_ref, q_ref, kn_ref, vn_ref, lfn_ref, *rest, pages, n_heads, hd):
    kt_refs = rest[:pages]
    vt_refs = rest[pages:2 * pages]
    lft_refs = rest[2 * pages:3 * pages]
    o_ref, m_ref, l_ref, acc_ref, carry_ref = rest[3 * pages:]
    g = pl.program_id(1)
    da = q_ref.shape[-1]
    psz = kt_refs[0].shape[-1]

    hrow = lax.broadcasted_iota(jnp.int32, (n_heads, da), 0)
    ccol = lax.broadcasted_iota(jnp.int32, (n_heads, da), 1)
    own = (ccol >= hrow * hd) & (ccol < (hrow + 1) * hd)
    qmat = jnp.where(own, q_ref[0], 0.0)
    qmat_bf = qmat.astype(BF16)
    cn = lfn_ref[0]

    @pl.when(g == 0)
    def _():
        m_ref[...] = jnp.sum(qmat * kn_ref[0], axis=1, keepdims=True)
        l_ref[...] = jnp.ones_like(l_ref)
        acc_ref[...] = jnp.broadcast_to(vn_ref[0], acc_ref.shape)
        carry_ref[...] = jnp.zeros_like(carry_ref)

    lane = lax.broadcasted_iota(jnp.int32, (n_heads, psz), 1)
    carry = carry_ref[...]
    logits = [None] * pages
    for r in reversed(range(pages)):
        s = _dot(qmat_bf, kt_refs[r][0].astype(BF16))
        lft = lft_refs[r][0]
        suf = jnp.where(lane < psz - 1, pltpu.roll(lft, psz - 1, 1), 0.0)
        sh = 1
        while sh < psz:
            suf = suf + jnp.where(lane < psz - sh, pltpu.roll(suf, psz - sh, 1), 0.0)
            sh *= 2
        logits[r] = s + (cn + carry) + suf
        carry = carry + suf[:, 0:1] + lft[:, 0:1]
    carry_ref[...] = carry
    lg = jnp.concatenate(logits, axis=1)
    m_old = m_ref[...]
    m_new = jnp.maximum(m_old, jnp.max(lg, axis=1, keepdims=True))
    alpha = jnp.exp(m_old - m_new)
    p = jnp.exp(lg - m_new)
    l_ref[...] = alpha * l_ref[...] + jnp.sum(p, axis=1, keepdims=True)
    m_ref[...] = m_new
    vt_all = jnp.concatenate([vt_refs[r][0].astype(BF16) for r in range(pages)], axis=1)
    acc_ref[...] = alpha * acc_ref[...] + _dot_nt(p.astype(BF16), vt_all)

    @pl.when(g == pl.num_programs(1) - 1)
    def _():
        o = jnp.where(own, acc_ref[...] / l_ref[...], 0.0)
        o_ref[0] = jnp.sum(o, axis=0, keepdims=True).astype(o_ref.dtype)


def _sample_attn(q, k_new, v_new, lf_new_t, cache_kt, cache_vt, cache_lft, page_table, n_heads, hd, pages=8):
    nb, _, da = q.shape
    n_pages = page_table.shape[1]
    psz = cache_kt.shape[-1]
    ng = n_pages // pages

    def page_map(r):
        return lambda b, g, pt: (pt[b, (ng - 1 - g) * pages + r], 0, 0)

    tok = pl.BlockSpec((1, 1, da), lambda b, g, pt: (b, 0, 0))
    grid_spec = pltpu.PrefetchScalarGridSpec(
        num_scalar_prefetch=1,
        grid=(nb, ng),
        in_specs=[tok, tok, tok, pl.BlockSpec((1, n_heads, 1), lambda b, g, pt: (b, 0, 0))]
        + [pl.BlockSpec((1, da, psz), page_map(r)) for r in range(pages)]
        + [pl.BlockSpec((1, da, psz), page_map(r)) for r in range(pages)]
        + [pl.BlockSpec((1, n_heads, psz), page_map(r)) for r in range(pages)],
        out_specs=tok,
        scratch_shapes=[
            pltpu.VMEM((n_heads, 1), F32),
            pltpu.VMEM((n_heads, 1), F32),
            pltpu.VMEM((n_heads, da), F32),
            pltpu.VMEM((n_heads, 1), F32),
        ],
    )
    return pl.pallas_call(
        functools.partial(_sample_attn_kernel, pages=pages, n_heads=n_heads, hd=hd),
        grid_spec=grid_spec,
        out_shape=jax.ShapeDtypeStruct((nb, 1, da), BF16),
        compiler_params=_params(40, ("arbitrary", "arbitrary")),
        name="sample_attn",
    )(page_table, q, k_new, v_new, lf_new_t, *([cache_kt] * pages), *([cache_vt] * pages), *([cache_lft] * pages))


def kernel(x_prompt, x_sample, c_prompt, c_sample, cache_k, cache_v, cache_logf, state_conv, page_table,
           w_ada, b_ada, g_mix, w_in, b_f, conv_w, w_out, g_mlp, w_up, w_down, g_final):
    depth = w_ada.shape[0]
    assert depth == 1, "single-layer trunk"
    nb, seq, d = x_prompt.shape
    ns, dec_seq, _ = x_sample.shape
    assert dec_seq == 1, "one new token per sampled sequence"
    n_heads = b_f.shape[-1]
    hd = cache_k.shape[-1]
    da = n_heads * hd
    dc = conv_w.shape[-1]
    n_pool, psz = cache_k.shape[1], cache_k.shape[2]
    scale = hd ** -0.5
    assert LANES % hd == 0 and w_in.shape[-1] == 3 * dc + 3 * da + n_heads + 2 * d

    wt = jnp.transpose(w_in[0])
    o_q, o_f, o_g = 3 * dc, 3 * dc + 3 * da, 3 * dc + 3 * da + n_heads
    wt_conv = wt[:o_q].astype(BF16)
    wt_qkvf = wt[o_q:o_g].astype(BF16)
    wt_cqkv = wt[:o_f].astype(BF16)
    wt_f = wt[o_f:o_g].astype(BF16)
    wt_gate = wt[o_g:].astype(BF16)
    w_out_bf = w_out[0].astype(BF16)
    w_up_bf = w_up[0].astype(BF16)
    w_down_bf = w_down[0].astype(BF16)
    g1 = g_mix[0].reshape(1, d)
    g2 = g_mlp[0].reshape(1, d)
    gf = g_final.reshape(1, d)
    cw = conv_w[0]

    mod = _modulation(jnp.concatenate([c_prompt, c_sample], axis=0), w_ada[0], b_ada[0])
    mod_p = [m.reshape(nb, 1, d) for m in jnp.split(mod[:nb], 6, axis=-1)]
    mod_s = [m.reshape(1, ns, d) for m in jnp.split(mod[nb:], 6, axis=-1)]

    sh_a, sc_a, gt_a, sh_m, sc_m, gt_m = mod_p
    a_p, conv_p = _prompt_conv(x_prompt, g1, sc_a, sh_a, wt_conv, cw)
    qt_p, kt_p, vt_p, lft_p = _prompt_qkv(x_prompt, g1, sc_a, sh_a, wt_qkvf, b_f[0].reshape(n_heads, 1), da, scale)
    crow = _prompt_cumsum(lft_p)
    b_p = _prompt_attn(qt_p, kt_p, vt_p, crow, hd)
    h_p, m_p = _mix(x_prompt, a_p, b_p, g1, sc_a, sh_a, gt_a, g2, sc_m, sh_m, wt_gate, w_out_bf, tm=256)
    y_prompt = _mlp(m_p, h_p, gt_m, gf, w_up_bf, w_down_bf, tm=512)

    sh_a, sc_a, gt_a, sh_m, sc_m, gt_m = mod_s
    xs = x_sample.reshape(ns, d)
    hist = state_conv[0]
    a_s, z_s, q_s, k_s, v_s, lf_s = _sample_proj(
        xs, g1, sc_a[0], sh_a[0], wt_cqkv, wt_f, b_f[0].reshape(1, n_heads), hist[:, 0], hist[:, 1], cw, scale)
    cache_kt = jnp.transpose(cache_k[0], (0, 2, 3, 1)).reshape(n_pool, da, psz)
    cache_vt = jnp.transpose(cache_v[0], (0, 2, 3, 1)).reshape(n_pool, da, psz)
    cache_lft = jnp.transpose(cache_logf[0], (0, 2, 1))
    b_s = _sample_attn(
        q_s.reshape(ns, 1, da), k_s.reshape(ns, 1, da), v_s.reshape(ns, 1, da), lf_s.reshape(ns, n_heads, 1),
        cache_kt, cache_vt, cache_lft, page_table, n_heads, hd)
    xs3 = xs.reshape(1, ns, d)
    h_s, m_s = _mix(xs3, a_s.reshape(1, ns, dc), b_s.reshape(1, ns, da), g1, sc_a, sh_a, gt_a, g2, sc_m, sh_m,
                    wt_gate, w_out_bf, tm=ns)
    y_sample = _mlp(m_s, h_s, gt_m, gf, w_up_bf, w_down_bf, tm=ns).reshape(ns, 1, d)

    def heads_last(xt):
        return jnp.transpose(xt.reshape(nb, n_heads, hd, seq), (0, 3, 1, 2))[None]

    return (
        y_prompt,
        y_sample,
        heads_last(kt_p),
        heads_last(vt_p),
        jnp.transpose(lft_p, (0, 2, 1))[None],
        conv_p.reshape(1, nb, 2, dc),
        k_s.reshape(1, ns, 1, n_heads, hd),
        v_s.reshape(1, ns, 1, n_heads, hd),
        lf_s.reshape(1, ns, 1, n_heads),
        jnp.stack([hist[:, 1], z_s], axis=1).reshape(1, ns, 2, dc),
    )
```

```python
import functools

import jax
import jax.numpy as jnp
from jax import lax
from jax.experimental import pallas as pl
from jax.experimental.pallas import tpu as pltpu

F32 = jnp.float32
BF16 = jnp.bfloat16
NORM_EPS = 1e-6
LANES = 128
MIB = 1024 * 1024
NEG_INF = float("-inf")
LOG2E = 1.4426950408889634


def _params(vmem_mib, semantics=None):
    return pltpu.CompilerParams(vmem_limit_bytes=vmem_mib * MIB, dimension_semantics=semantics)


def _dot(a, b):
    return jnp.dot(a, b, preferred_element_type=F32)


def _dot_nt(a, b):
    return lax.dot_general(a, b, (((1,), (1,)), ((), ())), preferred_element_type=F32)


def _rmsnorm(x, g):
    return (x * lax.rsqrt(jnp.mean(x * x, axis=-1, keepdims=True) + NORM_EPS)) * g


def _norm_mod(x, g, sc, sh):
    return _rmsnorm(x, g) * (1.0 + sc) + sh


def _resident(shape, index_map):
    return pl.BlockSpec(shape, index_map, pipeline_mode=pl.Buffered(1))


def _mod_spec(mod, tm):
    d = mod.shape[-1]
    if mod.shape[1] == 1:
        return pl.BlockSpec((1, 1, d), lambda b, t, *_: (b, 0, 0))
    return pl.BlockSpec((1, tm, d), lambda b, t, *_: (b, t, 0))


def _mod_kernel(c_ref, w_ref, b_ref, o_ref):
    c = c_ref[...]
    s = (c * jax.nn.sigmoid(c)).astype(BF16)
    o_ref[...] = _dot(s, w_ref[...].astype(BF16)) + b_ref[...]


def _modulation(c, w_ada, b_ada, tn=1024):
    n, d = c.shape
    dn = w_ada.shape[1]
    return pl.pallas_call(
        _mod_kernel,
        grid=(dn // tn,),
        in_specs=[
            pl.BlockSpec((n, d), lambda j: (0, 0)),
            pl.BlockSpec((d, tn), lambda j: (0, j)),
            pl.BlockSpec((1, tn), lambda j: (0, j)),
        ],
        out_specs=pl.BlockSpec((n, tn), lambda j: (0, j)),
        out_shape=jax.ShapeDtypeStruct((n, dn), F32),
        compiler_params=_params(40),
        name="modulation",
    )(c, w_ada, b_ada.reshape(1, dn))


def _conv_kernel(x_ref, g_ref, sc_ref, sh_ref, w_ref, cw_ref, a_ref, st_ref, carry_ref):
    t = pl.program_id(1)

    @pl.when(t == 0)
    def _():
        carry_ref[...] = jnp.zeros_like(carry_ref)

    u = _norm_mod(x_ref[0], g_ref[...], sc_ref[0], sh_ref[0]).astype(BF16)
    dc = cw_ref.shape[1]
    r = _dot_nt(u, w_ref[...])
    cb, cc, cx = r[:, :dc], r[:, dc:2 * dc], r[:, 2 * dc:]
    z = cc * cx
    tm = z.shape[0]
    prev = carry_ref[...]
    row = lax.broadcasted_iota(jnp.int32, z.shape, 0)
    z1 = jnp.where(row >= 1, pltpu.roll(z, 1, 0), prev[7:8])
    z2 = jnp.where(row >= 2, pltpu.roll(z, 2, 0), jnp.where(row == 1, prev[7:8], prev[6:7]))
    cw = cw_ref[...]
    y = cw[0:1] * z2 + cw[1:2] * z1 + cw[2:3] * z
    a_ref[0] = (cb * y).astype(BF16)
    carry_ref[...] = z[tm - 8:tm]
    st_ref[0] = z[tm - 2:tm]


def _prompt_conv(x, g, sc, sh, wt_conv, conv_w, tm=512):
    b, t, d = x.shape
    dc = conv_w.shape[1]
    return pl.pallas_call(
        _conv_kernel,
        grid=(b, t // tm),
        in_specs=[
            pl.BlockSpec((1, tm, d), lambda i, j: (i, j, 0)),
            _resident((1, d), lambda i, j: (0, 0)),
            _mod_spec(sc, tm),
            _mod_spec(sh, tm),
            _resident((3 * dc, d), lambda i, j: (0, 0)),
            _resident((3, dc), lambda i, j: (0, 0)),
        ],
        out_specs=[
            pl.BlockSpec((1, tm, dc), lambda i, j: (i, j, 0)),
            pl.BlockSpec((1, 2, dc), lambda i, j: (i, 0, 0)),
        ],
        out_shape=[
            jax.ShapeDtypeStruct((b, t, dc), BF16),
            jax.ShapeDtypeStruct((b, 2, dc), F32),
        ],
        scratch_shapes=[pltpu.VMEM((8, dc), F32)],
        compiler_params=_params(56, ("arbitrary", "arbitrary")),
        name="prompt_conv",
    )(x, g, sc, sh, wt_conv, conv_w)


def _qkv_kernel(x_ref, g_ref, sc_ref, sh_ref, w_ref, bf_ref, qt_ref, kt_ref, vt_ref, lft_ref, ktok_ref, *, scale):
    u = _norm_mod(x_ref[0], g_ref[...], sc_ref[0], sh_ref[0]).astype(BF16)
    da = qt_ref.shape[1]
    r = _dot_nt(w_ref[...], u)
    qt_ref[0] = (r[:da] * scale).astype(BF16)
    kt = r[da:2 * da]
    kt_ref[0] = kt
    ktok_ref[0] = kt.T.astype(BF16)
    vt_ref[0] = r[2 * da:3 * da]
    lft_ref[0] = jax.nn.log_sigmoid(r[3 * da:] + bf_ref[...])


def _prompt_qkv(x, g, sc, sh, wt_qkvf, bf_col, da, scale, tm=512):
    b, t, d = x.shape
    n_heads = bf_col.shape[0]
    feat = lambda i, j: (i, 0, j)
    return pl.pallas_call(
        functools.partial(_qkv_kernel, scale=scale),
        grid=(b, t // tm),
        in_specs=[
            pl.BlockSpec((1, tm, d), lambda i, j: (i, j, 0)),
            _resident((1, d), lambda i, j: (0, 0)),
            _mod_spec(sc, tm),
            _mod_spec(sh, tm),
            _resident((3 * da + n_heads, d), lambda i, j: (0, 0)),
            _resident((n_heads, 1), lambda i, j: (0, 0)),
        ],
        out_specs=[
            pl.BlockSpec((1, da, tm), feat),
            pl.BlockSpec((1, da, tm), feat),
            pl.BlockSpec((1, da, tm), feat),
            pl.BlockSpec((1, n_heads, tm), feat),
            pl.BlockSpec((1, tm, da), lambda i, j: (i, j, 0)),
        ],
        out_shape=[
            jax.ShapeDtypeStruct((b, da, t), BF16),
            jax.ShapeDtypeStruct((b, da, t), F32),
            jax.ShapeDtypeStruct((b, da, t), F32),
            jax.ShapeDtypeStruct((b, n_heads, t), F32),
            jax.ShapeDtypeStruct((b, t, da), BF16),
        ],
        compiler_params=_params(58, ("arbitrary", "arbitrary")),
        name="prompt_qkv",
    )(x, g, sc, sh, wt_qkvf, bf_col)


def _split3(x):
    hi = x.astype(BF16)
    r = x - hi.astype(F32)
    mid = r.astype(BF16)
    lo = (r - mid.astype(F32)).astype(BF16)
    return hi, mid, lo


def _decay_bias_kernel(x_ref, o_ref):
    x = x_ref[0]
    n_heads, t = x.shape
    lane = lax.broadcasted_iota(jnp.int32, x.shape, 1)
    sh = 1
    while sh < t:
        x = x + jnp.where(lane >= sh, pltpu.roll(x, sh, 1), 0.0)
        sh *= 2
    pieces = _split3(x * (-LOG2E))
    rows = [p[h:h + 1].astype(F32) for h in range(n_heads) for p in pieces]
    rows.append(jnp.zeros((LANES - 3 * n_heads, t), F32))
    o_ref[0] = jnp.concatenate(rows, axis=0).T.astype(BF16)


def _prompt_decay_bias(lft):
    b, n_heads, t = lft.shape
    assert 3 * n_heads <= LANES
    return pl.pallas_call(
        _decay_bias_kernel,
        grid=(b,),
        in_specs=[pl.BlockSpec((1, n_heads, t), lambda i: (i, 0, 0))],
        out_specs=pl.BlockSpec((1, t, LANES), lambda i: (i, 0, 0)),
        out_shape=jax.ShapeDtypeStruct((b, t, LANES), BF16),
        name="prompt_decay_bias",
    )(lft)


def _attn_kernel(qt_ref, k_ref, cb_ref, vt_ref, o_ref, vt_sc, *, hd):
    hp = pl.program_id(1)
    qi = pl.program_id(2)
    nh = LANES // hd
    t = k_ref.shape[1]
    tq = qt_ref.shape[2]

    def own(rows, hh):
        return (rows >= hh * hd) & (rows < (hh + 1) * hd)

    def ones_row(hh):
        return ((hh + 1) % nh) * hd

    @pl.when(qi == 0)
    def _():
        vt = vt_ref[0].astype(BF16)
        frow = lax.broadcasted_iota(jnp.int32, vt.shape, 0)
        for hh in range(nh):
            vh = jnp.where(own(frow, hh), vt, jnp.zeros_like(vt))
            vt_sc[hh] = jnp.where(frow == ones_row(hh), jnp.ones_like(vt), vh)

    qt = qt_ref[0]
    frow = lax.broadcasted_iota(jnp.int32, qt.shape, 0)
    qts = []
    for hh in range(nh):
        h = hp * nh + hh
        qh = jnp.where(own(frow, hh), qt, jnp.zeros_like(qt))
        pick = jnp.where((frow >= 3 * h) & (frow < 3 * h + 3), 1.0, 0.0).astype(BF16)
        qts.append(jnp.concatenate([qh, pick], axis=0))
    causal = lax.broadcasted_iota(jnp.int32, (tq, tq), 0) <= lax.broadcasted_iota(jnp.int32, (tq, tq), 1)

    def keys(lo, hi):
        return jnp.concatenate([k_ref[0, lo:hi, :], cb_ref[0, lo:hi, :]], axis=1)

    for c in range(t // tq):
        @pl.when(qi == c)
        def _(c=c):
            top = c * tq
            k_diag = keys(top, top + tq)
            s_diag = [jnp.where(causal, _dot(k_diag, qts[hh]), NEG_INF) for hh in range(nh)]
            if c > 0:
                k_top = keys(0, top)
                s_top = [_dot(k_top, qts[hh]) for hh in range(nh)]
            ot = None
            for hh in range(nh):
                m = jnp.max(s_diag[hh], axis=0, keepdims=True)
                if c > 0:
                    m = jnp.maximum(m, jnp.max(s_top[hh], axis=0, keepdims=True))
                acc = _dot(vt_sc[hh, :, top:top + tq], jnp.exp2(s_diag[hh] - m).astype(BF16))
                if c > 0:
                    acc = acc + _dot(vt_sc[hh, :, 0:top], jnp.exp2(s_top[hh] - m).astype(BF16))
                r = ones_row(hh)
                oth = acc / acc[r:r + 1]
                ot = oth if ot is None else jnp.where(own(frow, hh), oth, ot)
            o_ref[0] = ot.T.astype(BF16)


def _prompt_attn(qt, ktok, cbias, vt, hd, tq=512):
    b, da, t = qt.shape
    nh = LANES // hd
    return pl.pallas_call(
        functools.partial(_attn_kernel, hd=hd),
        grid=(b, da // LANES, t // tq),
        in_specs=[
            pl.BlockSpec((1, LANES, tq), lambda n, h, i: (n, h, i)),
            pl.BlockSpec((1, t, LANES), lambda n, h, i: (n, 0, h)),
            pl.BlockSpec((1, t, LANES), lambda n, h, i: (n, 0, 0)),
            pl.BlockSpec((1, LANES, t), lambda n, h, i: (n, h, 0)),
        ],
        out_specs=pl.BlockSpec((1, tq, LANES), lambda n, h, i: (n, i, h)),
        out_shape=jax.ShapeDtypeStruct((b, t, da), BF16),
        scratch_shapes=[pltpu.VMEM((nh, LANES, t), BF16)],
        compiler_params=_params(48, ("arbitrary", "arbitrary", "arbitrary")),
        name="prompt_attn",
    )(qt, ktok, cbias, vt)


def _mix_kernel(x_ref, a_ref, b_ref, g1_ref, sca_ref, sha_ref, gta_ref, g2_ref, scm_ref, shm_ref,
                wg_ref, wo_ref, h_ref, m_ref, *, tc):
    x = x_ref[0]
    d = x.shape[1]
    dc = a_ref.shape[-1]
    u = _norm_mod(x, g1_ref[...], sca_ref[0], sha_ref[0]).astype(BF16)
    a = a_ref[0]
    b = b_ref[0]
    gta = gta_ref[0]
    for c in range(d // tc):
        lo, hi = c * tc, (c + 1) * tc
        ga = _dot_nt(u, wg_ref[lo:hi, :])
        gb = _dot_nt(u, wg_ref[d + lo:d + hi, :])
        pa = _dot(a, wo_ref[:dc, lo:hi])
        pb = _dot(b, wo_ref[dc:, lo:hi])
        mixed = jax.nn.sigmoid(ga) * pa + jax.nn.sigmoid(gb) * pb
        h_ref[0, :, lo:hi] = x[:, lo:hi] + gta[:, lo:hi] * mixed
    m_ref[0] = _norm_mod(h_ref[0], g2_ref[...], scm_ref[0], shm_ref[0]).astype(BF16)


def _mix(x, a, bb, g1, sca, sha, gta, g2, scm, shm, wt_gate, w_out_bf, tm, tc=512):
    b, t, d = x.shape
    dc = a.shape[-1]
    da = bb.shape[-1]
    row = lambda i, j: (i, j, 0)
    const = lambda i, j: (0, 0)
    return pl.pallas_call(
        functools.partial(_mix_kernel, tc=tc),
        grid=(b, t // tm),
        in_specs=[
            pl.BlockSpec((1, tm, d), row),
            pl.BlockSpec((1, tm, dc), row),
            pl.BlockSpec((1, tm, da), row),
            _resident((1, d), const),
            _mod_spec(sca, tm), _mod_spec(sha, tm), _mod_spec(gta, tm),
            _resident((1, d), const),
            _mod_spec(scm, tm), _mod_spec(shm, tm),
            _resident((2 * d, d), const),
            _resident((dc + da, d), const),
        ],
        out_specs=[pl.BlockSpec((1, tm, d), row), pl.BlockSpec((1, tm, d), row)],
        out_shape=[jax.ShapeDtypeStruct((b, t, d), F32), jax.ShapeDtypeStruct((b, t, d), BF16)],
        compiler_params=_params(56, ("arbitrary", "arbitrary")),
        name="mix",
    )(x, a, bb, g1, sca, sha, gta, g2, scm, shm, wt_gate, w_out_bf)


def _mlp_kernel(m_ref, h_ref, gt_ref, gf_ref, wu_ref, wd_ref, y_ref, acc_ref):
    f = pl.program_id(2)

    @pl.when(f == 0)
    def _():
        acc_ref[...] = jnp.zeros_like(acc_ref)

    hid = _dot(m_ref[0], wu_ref[...])
    hid = jnp.square(jnp.maximum(hid, 0.0)).astype(BF16)
    acc_ref[...] += _dot(hid, wd_ref[...])

    @pl.when(f == pl.num_programs(2) - 1)
    def _():
        h2 = h_ref[0] + gt_ref[0] * acc_ref[...]
        y_ref[0] = _rmsnorm(h2, gf_ref[...])


def _mlp(m, h, gtm, g_final, w_up_bf, w_down_bf, tm, tf=1024):
    b, t, d = h.shape
    dff = w_up_bf.shape[1]
    row = lambda i, j, f: (i, j, 0)
    return pl.pallas_call(
        _mlp_kernel,
        grid=(b, t // tm, dff // tf),
        in_specs=[
            pl.BlockSpec((1, tm, d), row),
            pl.BlockSpec((1, tm, d), row),
            _mod_spec(gtm, tm),
            _resident((1, d), lambda i, j, f: (0, 0)),
            pl.BlockSpec((d, tf), lambda i, j, f: (0, f)),
            pl.BlockSpec((tf, d), lambda i, j, f: (f, 0)),
        ],
        out_specs=pl.BlockSpec((1, tm, d), row),
        out_shape=jax.ShapeDtypeStruct((b, t, d), F32),
        scratch_shapes=[pltpu.VMEM((tm, d), F32)],
        compiler_params=_params(52, ("arbitrary", "arbitrary", "arbitrary")),
        name="mlp",
    )(m, h, gtm, g_final, w_up_bf, w_down_bf)


def _sample_proj_kernel(x_ref, g_ref, sc_ref, sh_ref, wb_ref, wc_ref, wx_ref, wq_ref, wk_ref, wv_ref,
                        wf_ref, bf_ref, h0_ref, h1_ref, cw_ref,
                        a_ref, z_ref, q_ref, k_ref, v_ref, lf_ref, *, scale):
    u = _norm_mod(x_ref[...], g_ref[...], sc_ref[...], sh_ref[...]).astype(BF16)
    cb = _dot_nt(u, wb_ref[...])
    z = _dot_nt(u, wc_ref[...]) * _dot_nt(u, wx_ref[...])
    cw = cw_ref[...]
    y = cw[0:1] * h0_ref[...] + cw[1:2] * h1_ref[...] + cw[2:3] * z
    a_ref[...] = (cb * y).astype(BF16)
    z_ref[...] = z
    q_ref[...] = _dot_nt(u, wq_ref[...]) * scale
    k_ref[...] = _dot_nt(u, wk_ref[...])
    v_ref[...] = _dot_nt(u, wv_ref[...])

    @pl.when(pl.program_id(0) == 0)
    def _():
        lf_ref[...] = jax.nn.log_sigmoid(_dot_nt(u, wf_ref[...]) + bf_ref[...])


def _sample_proj(x, g, sc, sh, wt_cqkv, wt_f, bf_row, hist0, hist1, conv_w, scale, tc=512):
    n, d = x.shape
    dc = conv_w.shape[1]
    n_heads = wt_f.shape[0]
    nc = dc // tc
    whole = lambda shape: pl.BlockSpec(shape, lambda c: (0, 0))
    cols = pl.BlockSpec((n, tc), lambda c: (0, c))
    wspec = lambda r: pl.BlockSpec((tc, d), lambda c, r=r: (r * nc + c, 0))
    return pl.pallas_call(
        functools.partial(_sample_proj_kernel, scale=scale),
        grid=(nc,),
        in_specs=[whole((n, d)), whole((1, d)), whole((n, d)), whole((n, d))]
        + [wspec(r) for r in range(6)]
        + [whole((n_heads, d)), whole((1, n_heads)), cols, cols, pl.BlockSpec((3, tc), lambda c: (0, c))],
        out_specs=[cols, cols, cols, cols, cols, whole((n, n_heads))],
        out_shape=[jax.ShapeDtypeStruct((n, dc), BF16)]
        + [jax.ShapeDtypeStruct((n, dc), F32)] * 4
        + [jax.ShapeDtypeStruct((n, n_heads), F32)],
        compiler_params=_params(40, ("arbitrary",)),
        name="sample_proj",
    )(x, g, sc, sh, *([wt_cqkv] * 6), wt_f, bf_row, hist0, hist1, conv_w)


def _sample_attn_kernel(pt_ref, q_ref, kn_ref, vn_ref, lfn_ref, *rest, pages, n_heads, hd):
    kt_refs = rest[:pages]
    vt_refs = rest[pages:2 * pages]
    lft_refs = rest[2 * pages:3 * pages]
    o_ref, m_ref, l_ref, acc_ref, carry_ref = rest[3 * pages:]
    g = pl.program_id(1)
    da = q_ref.shape[-1]
    psz = kt_refs[0].shape[-1]

    hrow = lax.broadcasted_iota(jnp.int32, (n_heads, da), 0)
    ccol = lax.broadcasted_iota(jnp.int32, (n_heads, da), 1)
    own = (ccol >= hrow * hd) & (ccol < (hrow + 1) * hd)
    qmat = jnp.where(own, q_ref[0], 0.0)
    qmat_bf = qmat.astype(BF16)
    cn = lfn_ref[0]

    @pl.when(g == 0)
    def _():
        m_ref[...] = jnp.sum(qmat * kn_ref[0], axis=1, keepdims=True)
        l_ref[...] = jnp.ones_like(l_ref)
        acc_ref[...] = jnp.broadcast_to(vn_ref[0], acc_ref.shape)
        carry_ref[...] = jnp.zeros_like(carry_ref)

    lane = lax.broadcasted_iota(jnp.int32, (n_heads, psz), 1)
    carry = carry_ref[...]
    logits = [None] * pages
    for r in reversed(range(pages)):
        s = _dot(qmat_bf, kt_refs[r][0].astype(BF16))
        lft = lft_refs[r][0]
        suf = jnp.where(lane < psz - 1, pltpu.roll(lft, psz - 1, 1), 0.0)
        sh = 1
        while sh < psz:
            suf = suf + jnp.where(lane < psz - sh, pltpu.roll(suf, psz - sh, 1), 0.0)
            sh *= 2
        logits[r] = s + (cn + carry) + suf
        carry = carry + suf[:, 0:1] + lft[:, 0:1]
    carry_ref[...] = carry
    lg = jnp.concatenate(logits, axis=1)
    m_old = m_ref[...]
    m_new = jnp.maximum(m_old, jnp.max(lg, axis=1, keepdims=True))
    alpha = jnp.exp(m_old - m_new)
    p = jnp.exp(lg - m_new)
    l_ref[...] = alpha * l_ref[...] + jnp.sum(p, axis=1, keepdims=True)
    m_ref[...] = m_new
    vt_all = jnp.concatenate([vt_refs[r][0].astype(BF16) for r in range(pages)], axis=1)
    acc_ref[...] = alpha * acc_ref[...] + _dot_nt(p.astype(BF16), vt_all)

    @pl.when(g == pl.num_programs(1) - 1)
    def _():
        o = jnp.where(own, acc_ref[...] / l_ref[...], 0.0)
        o_ref[0] = jnp.sum(o, axis=0, keepdims=True).astype(o_ref.dtype)


def _sample_attn(q, k_new, v_new, lf_new_t, cache_kt, cache_vt, cache_lft, page_table, n_heads, hd, pages=16):
    nb, _, da = q.shape
    n_pages = page_table.shape[1]
    psz = cache_kt.shape[-1]
    ng = n_pages // pages

    def page_map(r):
        return lambda b, g, pt: (pt[b, (ng - 1 - g) * pages + r], 0, 0)

    tok = pl.BlockSpec((1, 1, da), lambda b, g, pt: (b, 0, 0))
    grid_spec = pltpu.PrefetchScalarGridSpec(
        num_scalar_prefetch=1,
        grid=(nb, ng),
        in_specs=[tok, tok, tok, pl.BlockSpec((1, n_heads, 1), lambda b, g, pt: (b, 0, 0))]
        + [pl.BlockSpec((1, da, psz), page_map(r)) for r in range(pages)]
        + [pl.BlockSpec((1, da, psz), page_map(r)) for r in range(pages)]
        + [pl.BlockSpec((1, n_heads, psz), page_map(r)) for r in range(pages)],
        out_specs=tok,
        scratch_shapes=[
            pltpu.VMEM((n_heads, 1), F32),
            pltpu.VMEM((n_heads, 1), F32),
            pltpu.VMEM((n_heads, da), F32),
            pltpu.VMEM((n_heads, 1), F32),
        ],
    )
    return pl.pallas_call(
        functools.partial(_sample_attn_kernel, pages=pages, n_heads=n_heads, hd=hd),
        grid_spec=grid_spec,
        out_shape=jax.ShapeDtypeStruct((nb, 1, da), BF16),
        compiler_params=_params(56, ("arbitrary", "arbitrary")),
        name="sample_attn",
    )(page_table, q, k_new, v_new, lf_new_t, *([cache_kt] * pages), *([cache_vt] * pages), *([cache_lft] * pages))


def kernel(x_prompt, x_sample, c_prompt, c_sample, cache_k, cache_v, cache_logf, state_conv, page_table,
           w_ada, b_ada, g_mix, w_in, b_f, conv_w, w_out, g_mlp, w_up, w_down, g_final):
    depth = w_ada.shape[0]
    assert depth == 1, "single-layer trunk"
    nb, seq, d = x_prompt.shape
    ns, dec_seq, _ = x_sample.shape
    assert dec_seq == 1, "one new token per sampled sequence"
    n_heads = b_f.shape[-1]
    hd = cache_k.shape[-1]
    da = n_heads * hd
    dc = conv_w.shape[-1]
    n_pool, psz = cache_k.shape[1], cache_k.shape[2]
    scale = hd ** -0.5
    assert LANES % hd == 0 and w_in.shape[-1] == 3 * dc + 3 * da + n_heads + 2 * d

    wt = jnp.transpose(w_in[0]).astype(BF16)
    o_q, o_f, o_g = 3 * dc, 3 * dc + 3 * da, 3 * dc + 3 * da + n_heads
    wt_qkvf = wt[o_q:o_g]
    wt_f = wt[o_f:o_g]
    wt_gate = wt[o_g:]
    w_out_bf = w_out[0].astype(BF16)
    w_up_bf = w_up[0].astype(BF16)
    w_down_bf = w_down[0].astype(BF16)
    g1 = g_mix[0].reshape(1, d)
    g2 = g_mlp[0].reshape(1, d)
    gf = g_final.reshape(1, d)
    cw = conv_w[0]

    mod = _modulation(jnp.concatenate([c_prompt, c_sample], axis=0), w_ada[0], b_ada[0])
    mod_p = [m.reshape(nb, 1, d) for m in jnp.split(mod[:nb], 6, axis=-1)]
    mod_s = [m.reshape(1, ns, d) for m in jnp.split(mod[nb:], 6, axis=-1)]

    sh_a, sc_a, gt_a, sh_m, sc_m, gt_m = mod_p
    a_p, conv_p = _prompt_conv(x_prompt, g1, sc_a, sh_a, wt, cw)
    qt_p, kt_p, vt_p, lft_p, ktok_p = _prompt_qkv(
        x_prompt, g1, sc_a, sh_a, wt_qkvf, b_f[0].reshape(n_heads, 1), da, scale * LOG2E)
    b_p = _prompt_attn(qt_p, ktok_p, _prompt_decay_bias(lft_p), vt_p, hd)
    h_p, m_p = _mix(x_prompt, a_p, b_p, g1, sc_a, sh_a, gt_a, g2, sc_m, sh_m, wt_gate, w_out_bf, tm=256)
    y_prompt = _mlp(m_p, h_p, gt_m, gf, w_up_bf, w_down_bf, tm=512)

    sh_a, sc_a, gt_a, sh_m, sc_m, gt_m = mod_s
    xs = x_sample.reshape(ns, d)
    hist = state_conv[0]
    a_s, z_s, q_s, k_s, v_s, lf_s = _sample_proj(
        xs, g1, sc_a[0], sh_a[0], wt, wt_f, b_f[0].reshape(1, n_heads), hist[:, 0], hist[:, 1], cw, scale)
    cache_kt = jnp.transpose(cache_k[0], (0, 2, 3, 1)).reshape(n_pool, da, psz)
    cache_vt = jnp.transpose(cache_v[0], (0, 2, 3, 1)).reshape(n_pool, da, psz)
    cache_lft = jnp.transpose(cache_logf[0], (0, 2, 1))
    b_s = _sample_attn(
        q_s.reshape(ns, 1, da), k_s.reshape(ns, 1, da), v_s.reshape(ns, 1, da), lf_s.reshape(ns, n_heads, 1),
        cache_kt, cache_vt, cache_lft, page_table, n_heads, hd)
    xs3 = xs.reshape(1, ns, d)
    h_s, m_s = _mix(xs3, a_s.reshape(1, ns, dc), b_s.reshape(1, ns, da), g1, sc_a, sh_a, gt_a, g2, sc_m, sh_m,
                    wt_gate, w_out_bf, tm=ns)
    y_sample = _mlp(m_s, h_s, gt_m, gf, w_up_bf, w_down_bf, tm=ns).reshape(ns, 1, d)

    def heads_last(xt):
        return jnp.transpose(xt.reshape(nb, n_heads, hd, seq), (0, 3, 1, 2))[None]

    return (
        y_prompt,
        y_sample,
        heads_last(kt_p),
        heads_last(vt_p),
        jnp.transpose(lft_p, (0, 2, 1))[None],
        conv_p.reshape(1, nb, 2, dc),
        k_s.reshape(1, ns, 1, n_heads, hd),
        v_s.reshape(1, ns, 1, n_heads, hd),
        lf_s.reshape(1, ns, 1, n_heads),
        jnp.stack([hist[:, 1], z_s], axis=1).reshape(1, ns, 2, dc),
    )
```

```python
import functools

import jax
import jax.numpy as jnp
from jax import lax
from jax.experimental import pallas as pl
from jax.experimental.pallas import tpu as pltpu

F32 = jnp.float32
BF16 = jnp.bfloat16
NORM_EPS = 1e-6
LANES = 128
MIB = 1024 * 1024
NEG_INF = float("-inf")
LOG2E = 1.4426950408889634


def _params(vmem_mib, semantics=None):
    return pltpu.CompilerParams(vmem_limit_bytes=vmem_mib * MIB, dimension_semantics=semantics)


def _dot(a, b):
    return jnp.dot(a, b, preferred_element_type=F32)


def _dot_nt(a, b):
    return lax.dot_general(a, b, (((1,), (1,)), ((), ())), preferred_element_type=F32)


def _rmsnorm(x, g):
    return (x * lax.rsqrt(jnp.mean(x * x, axis=-1, keepdims=True) + NORM_EPS)) * g


def _norm_mod(x, g, sc, sh):
    return _rmsnorm(x, g) * (1.0 + sc) + sh


def _resident(shape, index_map):
    return pl.BlockSpec(shape, index_map, pipeline_mode=pl.Buffered(1))


def _mod_spec(mod, tm):
    d = mod.shape[-1]
    if mod.shape[1] == 1:
        return pl.BlockSpec((1, 1, d), lambda b, t, *_: (b, 0, 0))
    return pl.BlockSpec((1, tm, d), lambda b, t, *_: (b, t, 0))


def _mod_kernel(c_ref, w_ref, b_ref, o_ref):
    c = c_ref[...]
    s = (c * jax.nn.sigmoid(c)).astype(BF16)
    o_ref[...] = _dot(s, w_ref[...].astype(BF16)) + b_ref[...]


def _modulation(c, w_ada, b_ada, tn=1024):
    n, d = c.shape
    dn = w_ada.shape[1]
    return pl.pallas_call(
        _mod_kernel,
        grid=(dn // tn,),
        in_specs=[
            pl.BlockSpec((n, d), lambda j: (0, 0)),
            pl.BlockSpec((d, tn), lambda j: (0, j)),
            pl.BlockSpec((1, tn), lambda j: (0, j)),
        ],
        out_specs=pl.BlockSpec((n, tn), lambda j: (0, j)),
        out_shape=jax.ShapeDtypeStruct((n, dn), F32),
        compiler_params=_params(40),
        name="modulation",
    )(c, w_ada, b_ada.reshape(1, dn))


def _conv_kernel(x_ref, g_ref, sc_ref, sh_ref, w_ref, cw_ref, a_ref, st_ref, carry_ref):
    t = pl.program_id(1)

    @pl.when(t == 0)
    def _():
        carry_ref[...] = jnp.zeros_like(carry_ref)

    u = _norm_mod(x_ref[0], g_ref[...], sc_ref[0], sh_ref[0]).astype(BF16)
    dc = cw_ref.shape[1]
    r = _dot_nt(u, w_ref[...])
    cb, cc, cx = r[:, :dc], r[:, dc:2 * dc], r[:, 2 * dc:]
    z = cc * cx
    tm = z.shape[0]
    prev = carry_ref[...]
    row = lax.broadcasted_iota(jnp.int32, z.shape, 0)
    z1 = jnp.where(row >= 1, pltpu.roll(z, 1, 0), prev[7:8])
    z2 = jnp.where(row >= 2, pltpu.roll(z, 2, 0), jnp.where(row == 1, prev[7:8], prev[6:7]))
    cw = cw_ref[...]
    y = cw[0:1] * z2 + cw[1:2] * z1 + cw[2:3] * z
    a_ref[0] = (cb * y).astype(BF16)
    carry_ref[...] = z[tm - 8:tm]
    st_ref[0] = z[tm - 2:tm]


def _prompt_conv(x, g, sc, sh, wt_conv, conv_w, tm=512):
    b, t, d = x.shape
    dc = conv_w.shape[1]
    return pl.pallas_call(
        _conv_kernel,
        grid=(b, t // tm),
        in_specs=[
            pl.BlockSpec((1, tm, d), lambda i, j: (i, j, 0)),
            _resident((1, d), lambda i, j: (0, 0)),
            _mod_spec(sc, tm),
            _mod_spec(sh, tm),
            _resident((3 * dc, d), lambda i, j: (0, 0)),
            _resident((3, dc), lambda i, j: (0, 0)),
        ],
        out_specs=[
            pl.BlockSpec((1, tm, dc), lambda i, j: (i, j, 0)),
            pl.BlockSpec((1, 2, dc), lambda i, j: (i, 0, 0)),
        ],
        out_shape=[
            jax.ShapeDtypeStruct((b, t, dc), BF16),
            jax.ShapeDtypeStruct((b, 2, dc), F32),
        ],
        scratch_shapes=[pltpu.VMEM((8, dc), F32)],
        compiler_params=_params(56, ("arbitrary", "arbitrary")),
        name="prompt_conv",
    )(x, g, sc, sh, wt_conv, conv_w)


def _qkv_kernel(x_ref, g_ref, sc_ref, sh_ref, w_ref, bf_ref, qt_ref, kt_ref, vt_ref, lft_ref, ktok_ref, *, scale):
    u = _norm_mod(x_ref[0], g_ref[...], sc_ref[0], sh_ref[0]).astype(BF16)
    da = qt_ref.shape[1]
    r = _dot_nt(w_ref[...], u)
    qt_ref[0] = (r[:da] * scale).astype(BF16)
    kt = r[da:2 * da]
    kt_ref[0] = kt
    ktok_ref[0] = kt.T.astype(BF16)
    vt_ref[0] = r[2 * da:3 * da]
    lft_ref[0] = jax.nn.log_sigmoid(r[3 * da:] + bf_ref[...])


def _prompt_qkv(x, g, sc, sh, wt_qkvf, bf_col, da, scale, tm=512):
    b, t, d = x.shape
    n_heads = bf_col.shape[0]
    feat = lambda i, j: (i, 0, j)
    return pl.pallas_call(
        functools.partial(_qkv_kernel, scale=scale),
        grid=(b, t // tm),
        in_specs=[
            pl.BlockSpec((1, tm, d), lambda i, j: (i, j, 0)),
            _resident((1, d), lambda i, j: (0, 0)),
            _mod_spec(sc, tm),
            _mod_spec(sh, tm),
            _resident((3 * da + n_heads, d), lambda i, j: (0, 0)),
            _resident((n_heads, 1), lambda i, j: (0, 0)),
        ],
        out_specs=[
            pl.BlockSpec((1, da, tm), feat),
            pl.BlockSpec((1, da, tm), feat),
            pl.BlockSpec((1, da, tm), feat),
            pl.BlockSpec((1, n_heads, tm), feat),
            pl.BlockSpec((1, tm, da), lambda i, j: (i, j, 0)),
        ],
        out_shape=[
            jax.ShapeDtypeStruct((b, da, t), BF16),
            jax.ShapeDtypeStruct((b, da, t), F32),
            jax.ShapeDtypeStruct((b, da, t), F32),
            jax.ShapeDtypeStruct((b, n_heads, t), F32),
            jax.ShapeDtypeStruct((b, t, da), BF16),
        ],
        compiler_params=_params(58, ("arbitrary", "arbitrary")),
        name="prompt_qkv",
    )(x, g, sc, sh, wt_qkvf, bf_col)


def _split3(x):
    hi = x.astype(BF16)
    r = x - hi.astype(F32)
    mid = r.astype(BF16)
    lo = (r - mid.astype(F32)).astype(BF16)
    return hi, mid, lo


def _decay_bias_kernel(x_ref, o_ref):
    x = x_ref[0]
    n_heads, t = x.shape
    lane = lax.broadcasted_iota(jnp.int32, x.shape, 1)
    sh = 1
    while sh < t:
        x = x + jnp.where(lane >= sh, pltpu.roll(x, sh, 1), 0.0)
        sh *= 2
    pieces = _split3(x * (-LOG2E))
    rows = [p[h:h + 1].astype(F32) for h in range(n_heads) for p in pieces]
    rows.append(jnp.zeros((LANES - 3 * n_heads, t), F32))
    o_ref[0] = jnp.concatenate(rows, axis=0).T.astype(BF16)


def _prompt_decay_bias(lft):
    b, n_heads, t = lft.shape
    assert 3 * n_heads <= LANES
    return pl.pallas_call(
        _decay_bias_kernel,
        grid=(b,),
        in_specs=[pl.BlockSpec((1, n_heads, t), lambda i: (i, 0, 0))],
        out_specs=pl.BlockSpec((1, t, LANES), lambda i: (i, 0, 0)),
        out_shape=jax.ShapeDtypeStruct((b, t, LANES), BF16),
        name="prompt_decay_bias",
    )(lft)


def _paged_logits(first, q_ref, kn_ref, vn_ref, lfn_ref, kt_refs, lft_refs,
                  m_ref, l_ref, acc_ref, carry_ref, *, n_heads, hd):
    pages = len(kt_refs)
    da = q_ref.shape[-1]
    psz = kt_refs[0].shape[-1]
    hrow = lax.broadcasted_iota(jnp.int32, (n_heads, da), 0)
    ccol = lax.broadcasted_iota(jnp.int32, (n_heads, da), 1)
    own = (ccol >= hrow * hd) & (ccol < (hrow + 1) * hd)
    qmat = jnp.where(own, q_ref[0], 0.0)
    qmat_bf = qmat.astype(BF16)
    cn = lfn_ref[0]
    m_old = jnp.where(first, jnp.sum(qmat * kn_ref[0], axis=1, keepdims=True), m_ref[...])
    l_old = jnp.where(first, 1.0, l_ref[...])
    acc_old = jnp.where(first, jnp.broadcast_to(vn_ref[0], acc_ref.shape), acc_ref[...])
    carry = jnp.where(first, 0.0, carry_ref[...])

    lane = lax.broadcasted_iota(jnp.int32, (n_heads, psz), 1)
    logits = [None] * pages
    for r in reversed(range(pages)):
        s = _dot(qmat_bf, kt_refs[r][0].astype(BF16))
        lft = lft_refs[r][0]
        suf = jnp.where(lane < psz - 1, pltpu.roll(lft, psz - 1, 1), 0.0)
        sh = 1
        while sh < psz:
            suf = suf + jnp.where(lane < psz - sh, pltpu.roll(suf, psz - sh, 1), 0.0)
            sh *= 2
        logits[r] = s + (cn + carry) + suf
        carry = carry + suf[:, 0:1] + lft[:, 0:1]
    carry_ref[...] = carry
    lg = jnp.concatenate(logits, axis=1)
    return own, lg, m_old, l_old, acc_old


def _paged_update(state, vt_refs, o_ref, m_ref, l_ref, acc_ref):
    own, lg, m_old, l_old, acc_old = state
    m_new = jnp.maximum(m_old, jnp.max(lg, axis=1, keepdims=True))
    alpha = jnp.exp(m_old - m_new)
    p = jnp.exp(lg - m_new)
    l_new = alpha * l_old + jnp.sum(p, axis=1, keepdims=True)
    vt_all = jnp.concatenate([ref[0].astype(BF16) for ref in vt_refs], axis=1)
    acc_new = alpha * acc_old + _dot_nt(p.astype(BF16), vt_all)
    m_ref[...] = m_new
    l_ref[...] = l_new
    acc_ref[...] = acc_new
    o_ref[0] = jnp.sum(jnp.where(own, acc_new / l_new, 0.0), axis=0, keepdims=True).astype(o_ref.dtype)


def _attn_kernel(pt_ref, qt_ref, k_ref, cb_ref, vt_ref, q_ref, kn_ref, vn_ref, lfn_ref, *rest,
                 hd, n_heads, pages, groups):
    kt_refs = rest[:pages]
    vts_refs = rest[pages:2 * pages]
    lft_refs = rest[2 * pages:3 * pages]
    o_ref, os_ref, vt_sc, m_ref, l_ref, acc_ref, carry_ref = rest[3 * pages:]
    del pt_ref
    hp = pl.program_id(1)
    qi = pl.program_id(2)
    step = (pl.program_id(0) * pl.num_programs(1) + hp) * pl.num_programs(2) + qi
    first = step % groups == 0
    nh = LANES // hd
    t = k_ref.shape[1]
    tq = qt_ref.shape[2]

    def sampled_logits():
        return _paged_logits(first, q_ref, kn_ref, vn_ref, lfn_ref, kt_refs, lft_refs,
                             m_ref, l_ref, acc_ref, carry_ref, n_heads=n_heads, hd=hd)

    def sampled_update(state):
        _paged_update(state, vts_refs, os_ref, m_ref, l_ref, acc_ref)

    def own(rows, hh):
        return (rows >= hh * hd) & (rows < (hh + 1) * hd)

    def ones_row(hh):
        return ((hh + 1) % nh) * hd

    @pl.when(qi == 0)
    def _():
        vt = vt_ref[0].astype(BF16)
        frow = lax.broadcasted_iota(jnp.int32, vt.shape, 0)
        for hh in range(nh):
            vh = jnp.where(own(frow, hh), vt, jnp.zeros_like(vt))
            vt_sc[hh] = jnp.where(frow == ones_row(hh), jnp.ones_like(vt), vh)

    qt = qt_ref[0]
    frow = lax.broadcasted_iota(jnp.int32, qt.shape, 0)
    qts = []
    for hh in range(nh):
        h = hp * nh + hh
        qh = jnp.where(own(frow, hh), qt, jnp.zeros_like(qt))
        pick = jnp.where((frow >= 3 * h) & (frow < 3 * h + 3), 1.0, 0.0).astype(BF16)
        qts.append(jnp.concatenate([qh, pick], axis=0))
    causal = lax.broadcasted_iota(jnp.int32, (tq, tq), 0) <= lax.broadcasted_iota(jnp.int32, (tq, tq), 1)

    def keys(lo, hi):
        return jnp.concatenate([k_ref[0, lo:hi, :], cb_ref[0, lo:hi, :]], axis=1)

    for c in range(t // tq):
        @pl.when(qi == c)
        def _(c=c):
            state = sampled_logits()
            top = c * tq
            k_diag = keys(top, top + tq)
            s_diag = [jnp.where(causal, _dot(k_diag, qts[hh]), NEG_INF) for hh in range(nh)]
            if c > 0:
                k_top = keys(0, top)
                s_top = [_dot(k_top, qts[hh]) for hh in range(nh)]
            sampled_update(state)
            ot = None
            for hh in range(nh):
                m = jnp.max(s_diag[hh], axis=0, keepdims=True)
                if c > 0:
                    m = jnp.maximum(m, jnp.max(s_top[hh], axis=0, keepdims=True))
                acc = _dot(vt_sc[hh, :, top:top + tq], jnp.exp2(s_diag[hh] - m).astype(BF16))
                if c > 0:
                    acc = acc + _dot(vt_sc[hh, :, 0:top], jnp.exp2(s_top[hh] - m).astype(BF16))
                r = ones_row(hh)
                oth = acc / acc[r:r + 1]
                ot = oth if ot is None else jnp.where(own(frow, hh), oth, ot)
            o_ref[0] = ot.T.astype(BF16)


def _attention(qt, ktok, cbias, vt, q_s, k_new, v_new, lf_new_t, cache_kt, cache_vt, cache_lft, page_table,
               hd, tq=512):
    b, da, t = qt.shape
    ns = q_s.shape[0]
    n_heads = lf_new_t.shape[1]
    n_pages = page_table.shape[1]
    psz = cache_kt.shape[-1]
    nh = LANES // hd
    grid = (b, da // LANES, t // tq)
    steps = grid[0] * grid[1] * grid[2]
    assert steps % ns == 0 and n_pages % (steps // ns) == 0, "sampled tokens must tile the prompt grid"
    groups = steps // ns
    pages = n_pages // groups

    def step_of(n, h, i):
        return (n * grid[1] + h) * grid[2] + i

    def page_map(r):
        def index(n, h, i, pt):
            s = step_of(n, h, i)
            return (pt[s // groups, (groups - 1 - s % groups) * pages + r], 0, 0)
        return index

    tok = pl.BlockSpec((1, 1, da), lambda n, h, i, pt: (step_of(n, h, i) // groups, 0, 0))
    grid_spec = pltpu.PrefetchScalarGridSpec(
        num_scalar_prefetch=1,
        grid=grid,
        in_specs=[
            pl.BlockSpec((1, LANES, tq), lambda n, h, i, pt: (n, h, i)),
            pl.BlockSpec((1, t, LANES), lambda n, h, i, pt: (n, 0, h)),
            pl.BlockSpec((1, t, LANES), lambda n, h, i, pt: (n, 0, 0)),
            pl.BlockSpec((1, LANES, t), lambda n, h, i, pt: (n, h, 0)),
            tok, tok, tok,
            pl.BlockSpec((1, n_heads, 1), lambda n, h, i, pt: (step_of(n, h, i) // groups, 0, 0)),
        ]
        + [pl.BlockSpec((1, da, psz), page_map(r)) for r in range(pages)]
        + [pl.BlockSpec((1, da, psz), page_map(r)) for r in range(pages)]
        + [pl.BlockSpec((1, n_heads, psz), page_map(r)) for r in range(pages)],
        out_specs=[pl.BlockSpec((1, tq, LANES), lambda n, h, i, pt: (n, i, h)), tok],
        scratch_shapes=[
            pltpu.VMEM((nh, LANES, t), BF16),
            pltpu.VMEM((n_heads, 1), F32),
            pltpu.VMEM((n_heads, 1), F32),
            pltpu.VMEM((n_heads, da), F32),
            pltpu.VMEM((n_heads, 1), F32),
        ],
    )
    return pl.pallas_call(
        functools.partial(_attn_kernel, hd=hd, n_heads=n_heads, pages=pages, groups=groups),
        grid_spec=grid_spec,
        out_shape=[jax.ShapeDtypeStruct((b, t, da), BF16), jax.ShapeDtypeStruct((ns, 1, da), BF16)],
        compiler_params=_params(56, ("arbitrary", "arbitrary", "arbitrary")),
        name="attention",
    )(page_table, qt, ktok, cbias, vt, q_s, k_new, v_new, lf_new_t,
      *([cache_kt] * pages), *([cache_vt] * pages), *([cache_lft] * pages))


def _mix_kernel(x_ref, a_ref, b_ref, g1_ref, sca_ref, sha_ref, gta_ref, g2_ref, scm_ref, shm_ref,
                wg_ref, wo_ref, h_ref, m_ref, *, tc):
    x = x_ref[0]
    d = x.shape[1]
    dc = a_ref.shape[-1]
    u = _norm_mod(x, g1_ref[...], sca_ref[0], sha_ref[0]).astype(BF16)
    a = a_ref[0]
    b = b_ref[0]
    gta = gta_ref[0]
    for c in range(d // tc):
        lo, hi = c * tc, (c + 1) * tc
        ga = _dot_nt(u, wg_ref[lo:hi, :])
        gb = _dot_nt(u, wg_ref[d + lo:d + hi, :])
        pa = _dot(a, wo_ref[:dc, lo:hi])
        pb = _dot(b, wo_ref[dc:, lo:hi])
        mixed = jax.nn.sigmoid(ga) * pa + jax.nn.sigmoid(gb) * pb
        h_ref[0, :, lo:hi] = x[:, lo:hi] + gta[:, lo:hi] * mixed
    m_ref[0] = _norm_mod(h_ref[0], g2_ref[...], scm_ref[0], shm_ref[0]).astype(BF16)


def _mix(x, a, bb, g1, sca, sha, gta, g2, scm, shm, wt_gate, w_out_bf, tm, tc=512):
    b, t, d = x.shape
    dc = a.shape[-1]
    da = bb.shape[-1]
    row = lambda i, j: (i, j, 0)
    const = lambda i, j: (0, 0)
    return pl.pallas_call(
        functools.partial(_mix_kernel, tc=tc),
        grid=(b, t // tm),
        in_specs=[
            pl.BlockSpec((1, tm, d), row),
            pl.BlockSpec((1, tm, dc), row),
            pl.BlockSpec((1, tm, da), row),
            _resident((1, d), const),
            _mod_spec(sca, tm), _mod_spec(sha, tm), _mod_spec(gta, tm),
            _resident((1, d), const),
            _mod_spec(scm, tm), _mod_spec(shm, tm),
            _resident((2 * d, d), const),
            _resident((dc + da, d), const),
        ],
        out_specs=[pl.BlockSpec((1, tm, d), row), pl.BlockSpec((1, tm, d), row)],
        out_shape=[jax.ShapeDtypeStruct((b, t, d), F32), jax.ShapeDtypeStruct((b, t, d), BF16)],
        compiler_params=_params(56, ("arbitrary", "arbitrary")),
        name="mix",
    )(x, a, bb, g1, sca, sha, gta, g2, scm, shm, wt_gate, w_out_bf)


def _mlp_kernel(m_ref, h_ref, gt_ref, gf_ref, wu_ref, wd_ref, y_ref, acc_ref):
    f = pl.program_id(2)

    @pl.when(f == 0)
    def _():
        acc_ref[...] = jnp.zeros_like(acc_ref)

    hid = _dot(m_ref[0], wu_ref[...])
    hid = jnp.square(jnp.maximum(hid, 0.0)).astype(BF16)
    acc_ref[...] += _dot(hid, wd_ref[...])

    @pl.when(f == pl.num_programs(2) - 1)
    def _():
        h2 = h_ref[0] + gt_ref[0] * acc_ref[...]
        y_ref[0] = _rmsnorm(h2, gf_ref[...])


def _mlp(m, h, gtm, g_final, w_up_bf, w_down_bf, tm, tf=1024):
    b, t, d = h.shape
    dff = w_up_bf.shape[1]
    row = lambda i, j, f: (i, j, 0)
    return pl.pallas_call(
        _mlp_kernel,
        grid=(b, t // tm, dff // tf),
        in_specs=[
            pl.BlockSpec((1, tm, d), row),
            pl.BlockSpec((1, tm, d), row),
            _mod_spec(gtm, tm),
            _resident((1, d), lambda i, j, f: (0, 0)),
            pl.BlockSpec((d, tf), lambda i, j, f: (0, f)),
            pl.BlockSpec((tf, d), lambda i, j, f: (f, 0)),
        ],
        out_specs=pl.BlockSpec((1, tm, d), row),
        out_shape=jax.ShapeDtypeStruct((b, t, d), F32),
        scratch_shapes=[pltpu.VMEM((tm, d), F32)],
        compiler_params=_params(52, ("arbitrary", "arbitrary", "arbitrary")),
        name="mlp",
    )(m, h, gtm, g_final, w_up_bf, w_down_bf)


def _sample_proj_kernel(x_ref, g_ref, sc_ref, sh_ref, wb_ref, wc_ref, wx_ref, wq_ref, wk_ref, wv_ref,
                        wf_ref, bf_ref, h0_ref, h1_ref, cw_ref,
                        a_ref, z_ref, q_ref, k_ref, v_ref, lf_ref, *, scale):
    u = _norm_mod(x_ref[...], g_ref[...], sc_ref[...], sh_ref[...]).astype(BF16)
    cb = _dot_nt(u, wb_ref[...])
    z = _dot_nt(u, wc_ref[...]) * _dot_nt(u, wx_ref[...])
    cw = cw_ref[...]
    y = cw[0:1] * h0_ref[...] + cw[1:2] * h1_ref[...] + cw[2:3] * z
    a_ref[...] = (cb * y).astype(BF16)
    z_ref[...] = z
    q_ref[...] = _dot_nt(u, wq_ref[...]) * scale
    k_ref[...] = _dot_nt(u, wk_ref[...])
    v_ref[...] = _dot_nt(u, wv_ref[...])

    @pl.when(pl.program_id(0) == 0)
    def _():
        lf_ref[...] = jax.nn.log_sigmoid(_dot_nt(u, wf_ref[...]) + bf_ref[...])


def _sample_proj(x, g, sc, sh, wt_cqkv, wt_f, bf_row, hist0, hist1, conv_w, scale, tc=512):
    n, d = x.shape
    dc = conv_w.shape[1]
    n_heads = wt_f.shape[0]
    nc = dc // tc
    whole = lambda shape: pl.BlockSpec(shape, lambda c: (0, 0))
    cols = pl.BlockSpec((n, tc), lambda c: (0, c))
    wspec = lambda r: pl.BlockSpec((tc, d), lambda c, r=r: (r * nc + c, 0))
    return pl.pallas_call(
        functools.partial(_sample_proj_kernel, scale=scale),
        grid=(nc,),
        in_specs=[whole((n, d)), whole((1, d)), whole((n, d)), whole((n, d))]
        + [wspec(r) for r in range(6)]
        + [whole((n_heads, d)), whole((1, n_heads)), cols, cols, pl.BlockSpec((3, tc), lambda c: (0, c))],
        out_specs=[cols, cols, cols, cols, cols, whole((n, n_heads))],
        out_shape=[jax.ShapeDtypeStruct((n, dc), BF16)]
        + [jax.ShapeDtypeStruct((n, dc), F32)] * 4
        + [jax.ShapeDtypeStruct((n, n_heads), F32)],
        compiler_params=_params(40, ("arbitrary",)),
        name="sample_proj",
    )(x, g, sc, sh, *([wt_cqkv] * 6), wt_f, bf_row, hist0, hist1, conv_w)


def kernel(x_prompt, x_sample, c_prompt, c_sample, cache_k, cache_v, cache_logf, state_conv, page_table,
           w_ada, b_ada, g_mix, w_in, b_f, conv_w, w_out, g_mlp, w_up, w_down, g_final):
    depth = w_ada.shape[0]
    assert depth == 1, "single-layer trunk"
    nb, seq, d = x_prompt.shape
    ns, dec_seq, _ = x_sample.shape
    assert dec_seq == 1, "one new token per sampled sequence"
    n_heads = b_f.shape[-1]
    hd = cache_k.shape[-1]
    da = n_heads * hd
    dc = conv_w.shape[-1]
    n_pool, psz = cache_k.shape[1], cache_k.shape[2]
    scale = hd ** -0.5
    assert LANES % hd == 0 and w_in.shape[-1] == 3 * dc + 3 * da + n_heads + 2 * d

    wt = jnp.transpose(w_in[0]).astype(BF16)
    o_q, o_f, o_g = 3 * dc, 3 * dc + 3 * da, 3 * dc + 3 * da + n_heads
    wt_qkvf = wt[o_q:o_g]
    wt_f = wt[o_f:o_g]
    wt_gate = wt[o_g:]
    w_out_bf = w_out[0].astype(BF16)
    w_up_bf = w_up[0].astype(BF16)
    w_down_bf = w_down[0].astype(BF16)
    g1 = g_mix[0].reshape(1, d)
    g2 = g_mlp[0].reshape(1, d)
    gf = g_final.reshape(1, d)
    cw = conv_w[0]

    mod = _modulation(jnp.concatenate([c_prompt, c_sample], axis=0), w_ada[0], b_ada[0])
    mod_p = [m.reshape(nb, 1, d) for m in jnp.split(mod[:nb], 6, axis=-1)]
    mod_s = [m.reshape(1, ns, d) for m in jnp.split(mod[nb:], 6, axis=-1)]

    sh_a, sc_a, gt_a, sh_m, sc_m, gt_m = mod_p
    a_p, conv_p = _prompt_conv(x_prompt, g1, sc_a, sh_a, wt, cw)
    qt_p, kt_p, vt_p, lft_p, ktok_p = _prompt_qkv(
        x_prompt, g1, sc_a, sh_a, wt_qkvf, b_f[0].reshape(n_heads, 1), da, scale * LOG2E)
    ssh_a, ssc_a, sgt_a, ssh_m, ssc_m, sgt_m = mod_s
    xs = x_sample.reshape(ns, d)
    hist = state_conv[0]
    a_s, z_s, q_s, k_s, v_s, lf_s = _sample_proj(
        xs, g1, ssc_a[0], ssh_a[0], wt, wt_f, b_f[0].reshape(1, n_heads), hist[:, 0], hist[:, 1], cw, scale)

    cache_kt = jnp.transpose(cache_k[0], (0, 2, 3, 1)).reshape(n_pool, da, psz)
    cache_vt = jnp.transpose(cache_v[0], (0, 2, 3, 1)).reshape(n_pool, da, psz)
    cache_lft = jnp.transpose(cache_logf[0], (0, 2, 1))
    b_p, b_s = _attention(
        qt_p, ktok_p, _prompt_decay_bias(lft_p), vt_p,
        q_s.reshape(ns, 1, da), k_s.reshape(ns, 1, da), v_s.reshape(ns, 1, da), lf_s.reshape(ns, n_heads, 1),
        cache_kt, cache_vt, cache_lft, page_table, hd)

    h_p, m_p = _mix(x_prompt, a_p, b_p, g1, sc_a, sh_a, gt_a, g2, sc_m, sh_m, wt_gate, w_out_bf, tm=256)
    y_prompt = _mlp(m_p, h_p, gt_m, gf, w_up_bf, w_down_bf, tm=512)
    xs3 = xs.reshape(1, ns, d)
    h_s, m_s = _mix(xs3, a_s.reshape(1, ns, dc), b_s.reshape(1, ns, da), g1, ssc_a, ssh_a, sgt_a, g2, ssc_m, ssh_m,
                    wt_gate, w_out_bf, tm=ns)
    y_sample = _mlp(m_s, h_s, sgt_m, gf, w_up_bf, w_down_bf, tm=ns).reshape(ns, 1, d)

    def heads_last(xt):
        return jnp.transpose(xt.reshape(nb, n_heads, hd, seq), (0, 3, 1, 2))[None]

    return (
        y_prompt,
        y_sample,
        heads_last(kt_p),
        heads_last(vt_p),
        jnp.transpose(lft_p, (0, 2, 1))[None],
        conv_p.reshape(1, nb, 2, dc),
        k_s.reshape(1, ns, 1, n_heads, hd),
        v_s.reshape(1, ns, 1, n_heads, hd),
        lf_s.reshape(1, ns, 1, n_heads),
        jnp.stack([hist[:, 1], z_s], axis=1).reshape(1, ns, 2, dc),
    )
```

```python
import functools

import jax
import jax.numpy as jnp
from jax import lax
from jax.experimental import pallas as pl
from jax.experimental.pallas import tpu as pltpu

F32 = jnp.float32
BF16 = jnp.bfloat16
NORM_EPS = 1e-6
LANES = 128
MIB = 1024 * 1024
NEG_INF = float("-inf")
LOG2E = 1.4426950408889634
ATTN_TQ = 512
MIX_TM = 256


def _params(vmem_mib, semantics=None):
    return pltpu.CompilerParams(vmem_limit_bytes=vmem_mib * MIB, dimension_semantics=semantics)


def _dot(a, b):
    return jnp.dot(a, b, preferred_element_type=F32)


def _dot_nt(a, b):
    return lax.dot_general(a, b, (((1,), (1,)), ((), ())), preferred_element_type=F32)


def _rmsnorm(x, g):
    return (x * lax.rsqrt(jnp.mean(x * x, axis=-1, keepdims=True) + NORM_EPS)) * g


def _norm_mod(x, g, sc, sh):
    return _rmsnorm(x, g) * (1.0 + sc) + sh


def _resident(shape, index_map):
    return pl.BlockSpec(shape, index_map, pipeline_mode=pl.Buffered(1))


def _mod_spec(mod, tm):
    d = mod.shape[-1]
    if mod.shape[1] == 1:
        return pl.BlockSpec((1, 1, d), lambda b, t, *_: (b, 0, 0))
    return pl.BlockSpec((1, tm, d), lambda b, t, *_: (b, t, 0))


def _mod_kernel(c_ref, w_ref, b_ref, o_ref):
    c = c_ref[...]
    s = (c * jax.nn.sigmoid(c)).astype(BF16)
    o_ref[...] = _dot(s, w_ref[...].astype(BF16)) + b_ref[...]


def _modulation(c, w_ada, b_ada, tn=1024):
    n, d = c.shape
    dn = w_ada.shape[1]
    return pl.pallas_call(
        _mod_kernel,
        grid=(dn // tn,),
        in_specs=[
            pl.BlockSpec((n, d), lambda j: (0, 0)),
            pl.BlockSpec((d, tn), lambda j: (0, j)),
            pl.BlockSpec((1, tn), lambda j: (0, j)),
        ],
        out_specs=pl.BlockSpec((n, tn), lambda j: (0, j)),
        out_shape=jax.ShapeDtypeStruct((n, dn), F32),
        compiler_params=_params(40),
        name="modulation",
    )(c, w_ada, b_ada.reshape(1, dn))


def _conv_kernel(x_ref, g_ref, sc_ref, sh_ref, w_ref, cw_ref, a_ref, st_ref, carry_ref):
    t = pl.program_id(1)

    @pl.when(t == 0)
    def _():
        carry_ref[...] = jnp.zeros_like(carry_ref)

    u = _norm_mod(x_ref[0], g_ref[...], sc_ref[0], sh_ref[0]).astype(BF16)
    dc = cw_ref.shape[1]
    r = _dot_nt(u, w_ref[...])
    cb, cc, cx = r[:, :dc], r[:, dc:2 * dc], r[:, 2 * dc:]
    z = cc * cx
    tm = z.shape[0]
    prev = carry_ref[...]
    row = lax.broadcasted_iota(jnp.int32, z.shape, 0)
    z1 = jnp.where(row >= 1, pltpu.roll(z, 1, 0), prev[7:8])
    z2 = jnp.where(row >= 2, pltpu.roll(z, 2, 0), jnp.where(row == 1, prev[7:8], prev[6:7]))
    cw = cw_ref[...]
    y = cw[0:1] * z2 + cw[1:2] * z1 + cw[2:3] * z
    a_ref[0] = (cb * y).astype(BF16)
    carry_ref[...] = z[tm - 8:tm]
    st_ref[0] = z[tm - 2:tm]


def _prompt_conv(x, g, sc, sh, wt_conv, conv_w, tm=512):
    b, t, d = x.shape
    dc = conv_w.shape[1]
    return pl.pallas_call(
        _conv_kernel,
        grid=(b, t // tm),
        in_specs=[
            pl.BlockSpec((1, tm, d), lambda i, j: (i, j, 0)),
            _resident((1, d), lambda i, j: (0, 0)),
            _mod_spec(sc, tm),
            _mod_spec(sh, tm),
            _resident((3 * dc, d), lambda i, j: (0, 0)),
            _resident((3, dc), lambda i, j: (0, 0)),
        ],
        out_specs=[
            pl.BlockSpec((1, tm, dc), lambda i, j: (i, j, 0)),
            pl.BlockSpec((1, 2, dc), lambda i, j: (i, 0, 0)),
        ],
        out_shape=[
            jax.ShapeDtypeStruct((b, t, dc), BF16),
            jax.ShapeDtypeStruct((b, 2, dc), F32),
        ],
        scratch_shapes=[pltpu.VMEM((8, dc), F32)],
        compiler_params=_params(56, ("arbitrary", "arbitrary")),
        name="prompt_conv",
    )(x, g, sc, sh, wt_conv, conv_w)


def _qkv_kernel(x_ref, g_ref, sc_ref, sh_ref, w_ref, bf_ref, qt_ref, kt_ref, vt_ref, lft_ref, ktok_ref, *, scale):
    u = _norm_mod(x_ref[0], g_ref[...], sc_ref[0], sh_ref[0]).astype(BF16)
    da = qt_ref.shape[1]
    r = _dot_nt(w_ref[...], u)
    qt_ref[0] = (r[:da] * scale).astype(BF16)
    kt = r[da:2 * da]
    kt_ref[0] = kt
    ktok_ref[0] = kt.T.astype(BF16)
    vt_ref[0] = r[2 * da:3 * da]
    lft_ref[0] = jax.nn.log_sigmoid(r[3 * da:] + bf_ref[...])


def _prompt_qkv(x, g, sc, sh, wt_qkvf, bf_col, da, scale, tm=512):
    b, t, d = x.shape
    n_heads = bf_col.shape[0]
    feat = lambda i, j: (i, 0, j)
    return pl.pallas_call(
        functools.partial(_qkv_kernel, scale=scale),
        grid=(b, t // tm),
        in_specs=[
            pl.BlockSpec((1, tm, d), lambda i, j: (i, j, 0)),
            _resident((1, d), lambda i, j: (0, 0)),
            _mod_spec(sc, tm),
            _mod_spec(sh, tm),
            _resident((3 * da + n_heads, d), lambda i, j: (0, 0)),
            _resident((n_heads, 1), lambda i, j: (0, 0)),
        ],
        out_specs=[
            pl.BlockSpec((1, da, tm), feat),
            pl.BlockSpec((1, da, tm), feat),
            pl.BlockSpec((1, da, tm), feat),
            pl.BlockSpec((1, n_heads, tm), feat),
            pl.BlockSpec((1, tm, da), lambda i, j: (i, j, 0)),
        ],
        out_shape=[
            jax.ShapeDtypeStruct((b, da, t), BF16),
            jax.ShapeDtypeStruct((b, da, t), F32),
            jax.ShapeDtypeStruct((b, da, t), F32),
            jax.ShapeDtypeStruct((b, n_heads, t), F32),
            jax.ShapeDtypeStruct((b, t, da), BF16),
        ],
        compiler_params=_params(58, ("arbitrary", "arbitrary")),
        name="prompt_qkv",
    )(x, g, sc, sh, wt_qkvf, bf_col)


def _split3(x):
    hi = x.astype(BF16)
    r = x - hi.astype(F32)
    mid = r.astype(BF16)
    lo = (r - mid.astype(F32)).astype(BF16)
    return hi, mid, lo


def _decay_bias_kernel(x_ref, o_ref):
    x = x_ref[0]
    n_heads, t = x.shape
    lane = lax.broadcasted_iota(jnp.int32, x.shape, 1)
    sh = 1
    while sh < t:
        x = x + jnp.where(lane >= sh, pltpu.roll(x, sh, 1), 0.0)
        sh *= 2
    pieces = _split3(x * (-LOG2E))
    rows = [p[h:h + 1].astype(F32) for h in range(n_heads) for p in pieces]
    rows.append(jnp.zeros((LANES - 3 * n_heads, t), F32))
    o_ref[0] = jnp.concatenate(rows, axis=0).T.astype(BF16)


def _prompt_decay_bias(lft):
    b, n_heads, t = lft.shape
    assert 3 * n_heads <= LANES
    return pl.pallas_call(
        _decay_bias_kernel,
        grid=(b,),
        in_specs=[pl.BlockSpec((1, n_heads, t), lambda i: (i, 0, 0))],
        out_specs=pl.BlockSpec((1, t, LANES), lambda i: (i, 0, 0)),
        out_shape=jax.ShapeDtypeStruct((b, t, LANES), BF16),
        name="prompt_decay_bias",
    )(lft)


def _paged_logits(first, q_ref, kn_ref, vn_ref, lfn_ref, kt_refs, lft_refs,
                  m_ref, l_ref, acc_ref, carry_ref, *, n_heads, hd):
    pages = len(kt_refs)
    da = q_ref.shape[-1]
    psz = kt_refs[0].shape[-1]
    hrow = lax.broadcasted_iota(jnp.int32, (n_heads, da), 0)
    ccol = lax.broadcasted_iota(jnp.int32, (n_heads, da), 1)
    own = (ccol >= hrow * hd) & (ccol < (hrow + 1) * hd)
    qmat = jnp.where(own, q_ref[0], 0.0)
    qmat_bf = qmat.astype(BF16)
    cn = lfn_ref[0]
    m_old = jnp.where(first, jnp.sum(qmat * kn_ref[0], axis=1, keepdims=True), m_ref[...])
    l_old = jnp.where(first, 1.0, l_ref[...])
    acc_old = jnp.where(first, jnp.broadcast_to(vn_ref[0], acc_ref.shape), acc_ref[...])
    carry = jnp.where(first, 0.0, carry_ref[...])

    lane = lax.broadcasted_iota(jnp.int32, (n_heads, psz), 1)
    logits = [None] * pages
    for r in reversed(range(pages)):
        s = _dot(qmat_bf, kt_refs[r][0].astype(BF16))
        lft = lft_refs[r][0]
        suf = jnp.where(lane < psz - 1, pltpu.roll(lft, psz - 1, 1), 0.0)
        sh = 1
        while sh < psz:
            suf = suf + jnp.where(lane < psz - sh, pltpu.roll(suf, psz - sh, 1), 0.0)
            sh *= 2
        logits[r] = s + (cn + carry) + suf
        carry = carry + suf[:, 0:1] + lft[:, 0:1]
    carry_ref[...] = carry
    lg = jnp.concatenate(logits, axis=1)
    return own, lg, m_old, l_old, acc_old


def _paged_update(state, vt_refs, o_ref, m_ref, l_ref, acc_ref):
    own, lg, m_old, l_old, acc_old = state
    m_new = jnp.maximum(m_old, jnp.max(lg, axis=1, keepdims=True))
    alpha = jnp.exp(m_old - m_new)
    p = jnp.exp(lg - m_new)
    l_new = alpha * l_old + jnp.sum(p, axis=1, keepdims=True)
    vt_all = jnp.concatenate([ref[0].astype(BF16) for ref in vt_refs], axis=1)
    acc_new = alpha * acc_old + _dot_nt(p.astype(BF16), vt_all)
    m_ref[...] = m_new
    l_ref[...] = l_new
    acc_ref[...] = acc_new
    o_ref[0] = jnp.sum(jnp.where(own, acc_new / l_new, 0.0), axis=0, keepdims=True).astype(o_ref.dtype)


def _attn_kernel(pt_ref, qta_ref, qtb_ref, k_ref, cb_ref, vt_ref, q_ref, kn_ref, vn_ref, lfn_ref, *rest,
                 hd, n_heads, pages, groups):
    kt_refs = rest[:pages]
    vts_refs = rest[pages:2 * pages]
    lft_refs = rest[2 * pages:3 * pages]
    o_ref, os_ref, vt_sc, m_ref, l_ref, acc_ref, carry_ref = rest[3 * pages:]
    del pt_ref
    hp = pl.program_id(1)
    qj = pl.program_id(2)
    step = (pl.program_id(0) * pl.num_programs(1) + hp) * pl.num_programs(2) + qj
    first = step % groups == 0
    nh = LANES // hd
    t = k_ref.shape[1]
    tq = qta_ref.shape[2]
    nq = t // tq

    def sampled_logits():
        return _paged_logits(first, q_ref, kn_ref, vn_ref, lfn_ref, kt_refs, lft_refs,
                             m_ref, l_ref, acc_ref, carry_ref, n_heads=n_heads, hd=hd)

    def sampled_update(state):
        _paged_update(state, vts_refs, os_ref, m_ref, l_ref, acc_ref)

    def own(rows, hh):
        return (rows >= hh * hd) & (rows < (hh + 1) * hd)

    def ones_row(hh):
        return ((hh + 1) % nh) * hd

    @pl.when(qj == 0)
    def _():
        vt = vt_ref[0].astype(BF16)
        frow = lax.broadcasted_iota(jnp.int32, vt.shape, 0)
        for hh in range(nh):
            vh = jnp.where(own(frow, hh), vt, jnp.zeros_like(vt))
            vt_sc[hh] = jnp.where(frow == ones_row(hh), jnp.ones_like(vt), vh)

    frow = lax.broadcasted_iota(jnp.int32, (LANES, tq), 0)
    causal = lax.broadcasted_iota(jnp.int32, (tq, tq), 0) <= lax.broadcasted_iota(jnp.int32, (tq, tq), 1)

    def queries(qt_ref):
        qt = qt_ref[0]
        out = []
        for hh in range(nh):
            h = hp * nh + hh
            qh = jnp.where(own(frow, hh), qt, jnp.zeros_like(qt))
            pick = jnp.where((frow >= 3 * h) & (frow < 3 * h + 3), 1.0, 0.0).astype(BF16)
            out.append(jnp.concatenate([qh, pick], axis=0))
        return out

    def keys(lo, hi):
        return jnp.concatenate([k_ref[0, lo:hi, :], cb_ref[0, lo:hi, :]], axis=1)

    def scores(c, qts):
        top = c * tq
        k_diag = keys(top, top + tq)
        s_diag = [jnp.where(causal, _dot(k_diag, qts[hh]), NEG_INF) for hh in range(nh)]
        s_top = [_dot(keys(0, top), qts[hh]) for hh in range(nh)] if c > 0 else None
        return s_diag, s_top

    def finish(c, s_diag, s_top, slot):
        top = c * tq
        ot = None
        for hh in range(nh):
            m = jnp.max(s_diag[hh], axis=0, keepdims=True)
            if c > 0:
                m = jnp.maximum(m, jnp.max(s_top[hh], axis=0, keepdims=True))
            acc = _dot(vt_sc[hh, :, top:top + tq], jnp.exp2(s_diag[hh] - m).astype(BF16))
            if c > 0:
                acc = acc + _dot(vt_sc[hh, :, 0:top], jnp.exp2(s_top[hh] - m).astype(BF16))
            r = ones_row(hh)
            oth = acc / acc[r:r + 1]
            ot = oth if ot is None else jnp.where(own(frow, hh), oth, ot)
        o_ref[0, slot * tq:(slot + 1) * tq, :] = ot.T.astype(BF16)

    for j in range(nq // 2):
        @pl.when(qj == j)
        def _(j=j):
            ca, cb = j, nq - 1 - j
            state = sampled_logits()
            sa = scores(ca, queries(qta_ref))
            sb = scores(cb, queries(qtb_ref))
            sampled_update(state)
            finish(ca, *sa, 0)
            finish(cb, *sb, 1)


def _paired_row_block(j, tm, tq, nq):
    per = tq // tm
    c = j // per
    pair = jnp.minimum(c, nq - 1 - c)
    slot = (c > pair).astype(jnp.int32)
    return (2 * pair + slot) * per + j % per


def _attention(qt, ktok, cbias, vt, q_s, k_new, v_new, lf_new_t, cache_kt, cache_vt, cache_lft, page_table,
               hd, tq=512):
    b, da, t = qt.shape
    ns = q_s.shape[0]
    n_heads = lf_new_t.shape[1]
    n_pages = page_table.shape[1]
    psz = cache_kt.shape[-1]
    nh = LANES // hd
    nq = t // tq
    assert nq % 2 == 0, "query tiles are processed in (j, last - j) pairs"
    grid = (b, da // LANES, nq // 2)
    steps = grid[0] * grid[1] * grid[2]
    assert steps % ns == 0 and n_pages % (steps // ns) == 0, "sampled tokens must tile the prompt grid"
    groups = steps // ns
    pages = n_pages // groups

    def step_of(n, h, i):
        return (n * grid[1] + h) * grid[2] + i

    def page_map(r):
        def index(n, h, i, pt):
            s = step_of(n, h, i)
            return (pt[s // groups, (groups - 1 - s % groups) * pages + r], 0, 0)
        return index

    tok = pl.BlockSpec((1, 1, da), lambda n, h, i, pt: (step_of(n, h, i) // groups, 0, 0))
    grid_spec = pltpu.PrefetchScalarGridSpec(
        num_scalar_prefetch=1,
        grid=grid,
        in_specs=[
            pl.BlockSpec((1, LANES, tq), lambda n, h, i, pt: (n, h, i)),
            pl.BlockSpec((1, LANES, tq), lambda n, h, i, pt: (n, h, nq - 1 - i)),
            pl.BlockSpec((1, t, LANES), lambda n, h, i, pt: (n, 0, h)),
            pl.BlockSpec((1, t, LANES), lambda n, h, i, pt: (n, 0, 0)),
            pl.BlockSpec((1, LANES, t), lambda n, h, i, pt: (n, h, 0)),
            tok, tok, tok,
            pl.BlockSpec((1, n_heads, 1), lambda n, h, i, pt: (step_of(n, h, i) // groups, 0, 0)),
        ]
        + [pl.BlockSpec((1, da, psz), page_map(r)) for r in range(pages)]
        + [pl.BlockSpec((1, da, psz), page_map(r)) for r in range(pages)]
        + [pl.BlockSpec((1, n_heads, psz), page_map(r)) for r in range(pages)],
        out_specs=[pl.BlockSpec((1, 2 * tq, LANES), lambda n, h, i, pt: (n, i, h)), tok],
        scratch_shapes=[
            pltpu.VMEM((nh, LANES, t), BF16),
            pltpu.VMEM((n_heads, 1), F32),
            pltpu.VMEM((n_heads, 1), F32),
            pltpu.VMEM((n_heads, da), F32),
            pltpu.VMEM((n_heads, 1), F32),
        ],
    )
    return pl.pallas_call(
        functools.partial(_attn_kernel, hd=hd, n_heads=n_heads, pages=pages, groups=groups),
        grid_spec=grid_spec,
        out_shape=[jax.ShapeDtypeStruct((b, t, da), BF16), jax.ShapeDtypeStruct((ns, 1, da), BF16)],
        compiler_params=_params(56, ("arbitrary", "arbitrary", "arbitrary")),
        name="attention",
    )(page_table, qt, qt, ktok, cbias, vt, q_s, k_new, v_new, lf_new_t,
      *([cache_kt] * pages), *([cache_vt] * pages), *([cache_lft] * pages))


def _mix_kernel(x_ref, a_ref, b_ref, g1_ref, sca_ref, sha_ref, gta_ref, g2_ref, scm_ref, shm_ref,
                wg_ref, wo_ref, h_ref, m_ref, *, tc):
    x = x_ref[0]
    d = x.shape[1]
    dc = a_ref.shape[-1]
    u = _norm_mod(x, g1_ref[...], sca_ref[0], sha_ref[0]).astype(BF16)
    a = a_ref[0]
    b = b_ref[0]
    gta = gta_ref[0]
    for c in range(d // tc):
        lo, hi = c * tc, (c + 1) * tc
        ga = _dot_nt(u, wg_ref[lo:hi, :])
        gb = _dot_nt(u, wg_ref[d + lo:d + hi, :])
        pa = _dot(a, wo_ref[:dc, lo:hi])
        pb = _dot(b, wo_ref[dc:, lo:hi])
        mixed = jax.nn.sigmoid(ga) * pa + jax.nn.sigmoid(gb) * pb
        h_ref[0, :, lo:hi] = x[:, lo:hi] + gta[:, lo:hi] * mixed
    m_ref[0] = _norm_mod(h_ref[0], g2_ref[...], scm_ref[0], shm_ref[0]).astype(BF16)


def _mix(x, a, bb, g1, sca, sha, gta, g2, scm, shm, wt_gate, w_out_bf, tm, tc=512, b_row_block=None):
    b, t, d = x.shape
    dc = a.shape[-1]
    da = bb.shape[-1]
    row = lambda i, j: (i, j, 0)
    brow = row if b_row_block is None else (lambda i, j: (i, b_row_block(j), 0))
    const = lambda i, j: (0, 0)
    return pl.pallas_call(
        functools.partial(_mix_kernel, tc=tc),
        grid=(b, t // tm),
        in_specs=[
            pl.BlockSpec((1, tm, d), row),
            pl.BlockSpec((1, tm, dc), row),
            pl.BlockSpec((1, tm, da), brow),
            _resident((1, d), const),
            _mod_spec(sca, tm), _mod_spec(sha, tm), _mod_spec(gta, tm),
            _resident((1, d), const),
            _mod_spec(scm, tm), _mod_spec(shm, tm),
            _resident((2 * d, d), const),
            _resident((dc + da, d), const),
        ],
        out_specs=[pl.BlockSpec((1, tm, d), row), pl.BlockSpec((1, tm, d), row)],
        out_shape=[jax.ShapeDtypeStruct((b, t, d), F32), jax.ShapeDtypeStruct((b, t, d), BF16)],
        compiler_params=_params(56, ("arbitrary", "arbitrary")),
        name="mix",
    )(x, a, bb, g1, sca, sha, gta, g2, scm, shm, wt_gate, w_out_bf)


def _mlp_kernel(m_ref, h_ref, gt_ref, gf_ref, wu_ref, wd_ref, y_ref, acc_ref):
    f = pl.program_id(2)

    @pl.when(f == 0)
    def _():
        acc_ref[...] = jnp.zeros_like(acc_ref)

    hid = _dot(m_ref[0], wu_ref[...])
    hid = jnp.square(jnp.maximum(hid, 0.0)).astype(BF16)
    acc_ref[...] += _dot(hid, wd_ref[...])

    @pl.when(f == pl.num_programs(2) - 1)
    def _():
        h2 = h_ref[0] + gt_ref[0] * acc_ref[...]
        y_ref[0] = _rmsnorm(h2, gf_ref[...])


def _mlp(m, h, gtm, g_final, w_up_bf, w_down_bf, tm, tf=1024):
    b, t, d = h.shape
    dff = w_up_bf.shape[1]
    row = lambda i, j, f: (i, j, 0)
    return pl.pallas_call(
        _mlp_kernel,
        grid=(b, t // tm, dff // tf),
        in_specs=[
            pl.BlockSpec((1, tm, d), row),
            pl.BlockSpec((1, tm, d), row),
            _mod_spec(gtm, tm),
            _resident((1, d), lambda i, j, f: (0, 0)),
            pl.BlockSpec((d, tf), lambda i, j, f: (0, f)),
            pl.BlockSpec((tf, d), lambda i, j, f: (f, 0)),
        ],
        out_specs=pl.BlockSpec((1, tm, d), row),
        out_shape=jax.ShapeDtypeStruct((b, t, d), F32),
        scratch_shapes=[pltpu.VMEM((tm, d), F32)],
        compiler_params=_params(52, ("arbitrary", "arbitrary", "arbitrary")),
        name="mlp",
    )(m, h, gtm, g_final, w_up_bf, w_down_bf)


def _sample_proj_kernel(x_ref, g_ref, sc_ref, sh_ref, wb_ref, wc_ref, wx_ref, wq_ref, wk_ref, wv_ref,
                        wf_ref, bf_ref, h0_ref, h1_ref, cw_ref,
                        a_ref, z_ref, q_ref, k_ref, v_ref, lf_ref, *, scale):
    u = _norm_mod(x_ref[...], g_ref[...], sc_ref[...], sh_ref[...]).astype(BF16)
    cb = _dot_nt(u, wb_ref[...])
    z = _dot_nt(u, wc_ref[...]) * _dot_nt(u, wx_ref[...])
    cw = cw_ref[...]
    y = cw[0:1] * h0_ref[...] + cw[1:2] * h1_ref[...] + cw[2:3] * z
    a_ref[...] = (cb * y).astype(BF16)
    z_ref[...] = z
    q_ref[...] = _dot_nt(u, wq_ref[...]) * scale
    k_ref[...] = _dot_nt(u, wk_ref[...])
    v_ref[...] = _dot_nt(u, wv_ref[...])

    @pl.when(pl.program_id(0) == 0)
    def _():
        lf_ref[...] = jax.nn.log_sigmoid(_dot_nt(u, wf_ref[...]) + bf_ref[...])


def _sample_proj(x, g, sc, sh, wt_cqkv, wt_f, bf_row, hist0, hist1, conv_w, scale, tc=512):
    n, d = x.shape
    dc = conv_w.shape[1]
    n_heads = wt_f.shape[0]
    nc = dc // tc
    whole = lambda shape: pl.BlockSpec(shape, lambda c: (0, 0))
    cols = pl.BlockSpec((n, tc), lambda c: (0, c))
    wspec = lambda r: pl.BlockSpec((tc, d), lambda c, r=r: (r * nc + c, 0))
    return pl.pallas_call(
        functools.partial(_sample_proj_kernel, scale=scale),
        grid=(nc,),
        in_specs=[whole((n, d)), whole((1, d)), whole((n, d)), whole((n, d))]
        + [wspec(r) for r in range(6)]
        + [whole((n_heads, d)), whole((1, n_heads)), cols, cols, pl.BlockSpec((3, tc), lambda c: (0, c))],
        out_specs=[cols, cols, cols, cols, cols, whole((n, n_heads))],
        out_shape=[jax.ShapeDtypeStruct((n, dc), BF16)]
        + [jax.ShapeDtypeStruct((n, dc), F32)] * 4
        + [jax.ShapeDtypeStruct((n, n_heads), F32)],
        compiler_params=_params(40, ("arbitrary",)),
        name="sample_proj",
    )(x, g, sc, sh, *([wt_cqkv] * 6), wt_f, bf_row, hist0, hist1, conv_w)


def kernel(x_prompt, x_sample, c_prompt, c_sample, cache_k, cache_v, cache_logf, state_conv, page_table,
           w_ada, b_ada, g_mix, w_in, b_f, conv_w, w_out, g_mlp, w_up, w_down, g_final):
    depth = w_ada.shape[0]
    assert depth == 1, "single-layer trunk"
    nb, seq, d = x_prompt.shape
    ns, dec_seq, _ = x_sample.shape
    assert dec_seq == 1, "one new token per sampled sequence"
    n_heads = b_f.shape[-1]
    hd = cache_k.shape[-1]
    da = n_heads * hd
    dc = conv_w.shape[-1]
    n_pool, psz = cache_k.shape[1], cache_k.shape[2]
    scale = hd ** -0.5
    assert LANES % hd == 0 and w_in.shape[-1] == 3 * dc + 3 * da + n_heads + 2 * d

    wt = jnp.transpose(w_in[0]).astype(BF16)
    o_q, o_f, o_g = 3 * dc, 3 * dc + 3 * da, 3 * dc + 3 * da + n_heads
    wt_qkvf = wt[o_q:o_g]
    wt_f = wt[o_f:o_g]
    wt_gate = wt[o_g:]
    w_out_bf = w_out[0].astype(BF16)
    w_up_bf = w_up[0].astype(BF16)
    w_down_bf = w_down[0].astype(BF16)
    g1 = g_mix[0].reshape(1, d)
    g2 = g_mlp[0].reshape(1, d)
    gf = g_final.reshape(1, d)
    cw = conv_w[0]

    mod = _modulation(jnp.concatenate([c_prompt, c_sample], axis=0), w_ada[0], b_ada[0])
    mod_p = [m.reshape(nb, 1, d) for m in jnp.split(mod[:nb], 6, axis=-1)]
    mod_s = [m.reshape(1, ns, d) for m in jnp.split(mod[nb:], 6, axis=-1)]

    sh_a, sc_a, gt_a, sh_m, sc_m, gt_m = mod_p
    a_p, conv_p = _prompt_conv(x_prompt, g1, sc_a, sh_a, wt, cw)
    qt_p, kt_p, vt_p, lft_p, ktok_p = _prompt_qkv(
        x_prompt, g1, sc_a, sh_a, wt_qkvf, b_f[0].reshape(n_heads, 1), da, scale * LOG2E)
    ssh_a, ssc_a, sgt_a, ssh_m, ssc_m, sgt_m = mod_s
    xs = x_sample.reshape(ns, d)
    hist = state_conv[0]
    a_s, z_s, q_s, k_s, v_s, lf_s = _sample_proj(
        xs, g1, ssc_a[0], ssh_a[0], wt, wt_f, b_f[0].reshape(1, n_heads), hist[:, 0], hist[:, 1], cw, scale)

    cache_kt = jnp.transpose(cache_k[0], (0, 2, 3, 1)).reshape(n_pool, da, psz)
    cache_vt = jnp.transpose(cache_v[0], (0, 2, 3, 1)).reshape(n_pool, da, psz)
    cache_lft = jnp.transpose(cache_logf[0], (0, 2, 1))
    b_p, b_s = _attention(
        qt_p, ktok_p, _prompt_decay_bias(lft_p), vt_p,
        q_s.reshape(ns, 1, da), k_s.reshape(ns, 1, da), v_s.reshape(ns, 1, da), lf_s.reshape(ns, n_heads, 1),
        cache_kt, cache_vt, cache_lft, page_table, hd, tq=ATTN_TQ)

    h_p, m_p = _mix(x_prompt, a_p, b_p, g1, sc_a, sh_a, gt_a, g2, sc_m, sh_m, wt_gate, w_out_bf, tm=MIX_TM,
                    b_row_block=functools.partial(_paired_row_block, tm=MIX_TM, tq=ATTN_TQ, nq=seq // ATTN_TQ))
    y_prompt = _mlp(m_p, h_p, gt_m, gf, w_up_bf, w_down_bf, tm=512)
    xs3 = xs.reshape(1, ns, d)
    h_s, m_s = _mix(xs3, a_s.reshape(1, ns, dc), b_s.reshape(1, ns, da), g1, ssc_a, ssh_a, sgt_a, g2, ssc_m, ssh_m,
                    wt_gate, w_out_bf, tm=ns)
    y_sample = _mlp(m_s, h_s, sgt_m, gf, w_up_bf, w_down_bf, tm=ns).reshape(ns, 1, d)

    def heads_last(xt):
        return jnp.transpose(xt.reshape(nb, n_heads, hd, seq), (0, 3, 1, 2))[None]

    return (
        y_prompt,
        y_sample,
        heads_last(kt_p),
        heads_last(vt_p),
        jnp.transpose(lft_p, (0, 2, 1))[None],
        conv_p.reshape(1, nb, 2, dc),
        k_s.reshape(1, ns, 1, n_heads, hd),
        v_s.reshape(1, ns, 1, n_heads, hd),
        lf_s.reshape(1, ns, 1, n_heads),
        jnp.stack([hist[:, 1], z_s], axis=1).reshape(1, ns, 2, dc),
    )
```

```python
import functools
import math

import jax
import jax.numpy as jnp
from jax import lax
from jax.experimental import pallas as pl
from jax.experimental.pallas import tpu as pltpu

F32 = jnp.float32
BF16 = jnp.bfloat16
NORM_EPS = 1e-6
LANES = 128
MIB = 1024 * 1024
V7X_VMEM_BYTES = 64 * MIB
NEG_INF = float("-inf")
LOG2E = 1.4426950408889634
ATTN_TQ = 512
MIX_TM = 256
MLP_TM = 512


def _nbytes(shape, dtype):
    return math.prod(shape) * jnp.dtype(dtype).itemsize


def _params(semantics, pipelined=(), resident=(), temporaries=()):
    need = 2 * sum(pipelined) + sum(resident) + sum(temporaries)
    assert need <= V7X_VMEM_BYTES, f"VMEM estimate {need / MIB:.1f} MiB exceeds the TensorCore's VMEM"
    return pltpu.CompilerParams(vmem_limit_bytes=int(need), dimension_semantics=semantics)


def _dot(a, b):
    return jnp.dot(a, b, preferred_element_type=F32)


def _dot_nt(a, b):
    return lax.dot_general(a, b, (((1,), (1,)), ((), ())), preferred_element_type=F32)


def _rmsnorm(x, g):
    return (x * lax.rsqrt(jnp.mean(x * x, axis=-1, keepdims=True) + NORM_EPS)) * g


def _norm_mod(x, g, sc, sh):
    return _rmsnorm(x, g) * (1.0 + sc) + sh


def _resident(shape, index_map):
    return pl.BlockSpec(shape, index_map, pipeline_mode=pl.Buffered(1))


def _mod_spec(mod, tm):
    d = mod.shape[-1]
    if mod.shape[1] == 1:
        return pl.BlockSpec((1, 1, d), lambda b, t, *_: (b, 0, 0))
    return pl.BlockSpec((1, tm, d), lambda b, t, *_: (b, t, 0))


def _mod_kernel(c_ref, w_ref, b_ref, o_ref):
    c = c_ref[...]
    s = (c * jax.nn.sigmoid(c)).astype(BF16)
    o_ref[...] = _dot(s, w_ref[...].astype(BF16)) + b_ref[...]


def _modulation(c, w_ada, b_ada, tn=1024):
    n, d = c.shape
    dn = w_ada.shape[1]
    return pl.pallas_call(
        _mod_kernel,
        grid=(dn // tn,),
        in_specs=[
            pl.BlockSpec((n, d), lambda j: (0, 0)),
            pl.BlockSpec((d, tn), lambda j: (0, j)),
            pl.BlockSpec((1, tn), lambda j: (0, j)),
        ],
        out_specs=pl.BlockSpec((n, tn), lambda j: (0, j)),
        out_shape=jax.ShapeDtypeStruct((n, dn), F32),
        compiler_params=_params(
            ("arbitrary",),
            pipelined=[_nbytes((n, d), F32), _nbytes((d, tn), F32), _nbytes((n, tn), F32)],
            temporaries=[_nbytes((d, tn), BF16), _nbytes((n, d), F32)]),
        name="modulation",
    )(c, w_ada, b_ada.reshape(1, dn))


def _conv_kernel(x_ref, g_ref, sc_ref, sh_ref, w_ref, cw_ref, a_ref, st_ref, carry_ref):
    t = pl.program_id(1)

    @pl.when(t == 0)
    def _():
        carry_ref[...] = jnp.zeros_like(carry_ref)

    u = _norm_mod(x_ref[0], g_ref[...], sc_ref[0], sh_ref[0]).astype(BF16)
    dc = cw_ref.shape[1]
    r = _dot_nt(u, w_ref[...])
    cb, cc, cx = r[:, :dc], r[:, dc:2 * dc], r[:, 2 * dc:]
    z = cc * cx
    tm = z.shape[0]
    prev = carry_ref[...]
    row = lax.broadcasted_iota(jnp.int32, z.shape, 0)
    z1 = jnp.where(row >= 1, pltpu.roll(z, 1, 0), prev[7:8])
    z2 = jnp.where(row >= 2, pltpu.roll(z, 2, 0), jnp.where(row == 1, prev[7:8], prev[6:7]))
    cw = cw_ref[...]
    y = cw[0:1] * z2 + cw[1:2] * z1 + cw[2:3] * z
    a_ref[0] = (cb * y).astype(BF16)
    carry_ref[...] = z[tm - 8:tm]
    st_ref[0] = z[tm - 2:tm]


def _prompt_conv(x, g, sc, sh, wt_conv, conv_w, tm=512):
    b, t, d = x.shape
    dc = conv_w.shape[1]
    return pl.pallas_call(
        _conv_kernel,
        grid=(b, t // tm),
        in_specs=[
            pl.BlockSpec((1, tm, d), lambda i, j: (i, j, 0)),
            _resident((1, d), lambda i, j: (0, 0)),
            _mod_spec(sc, tm),
            _mod_spec(sh, tm),
            _resident((3 * dc, d), lambda i, j: (0, 0)),
            _resident((3, dc), lambda i, j: (0, 0)),
        ],
        out_specs=[
            pl.BlockSpec((1, tm, dc), lambda i, j: (i, j, 0)),
            pl.BlockSpec((1, 2, dc), lambda i, j: (i, 0, 0)),
        ],
        out_shape=[
            jax.ShapeDtypeStruct((b, t, dc), BF16),
            jax.ShapeDtypeStruct((b, 2, dc), F32),
        ],
        scratch_shapes=[pltpu.VMEM((8, dc), F32)],
        compiler_params=_params(
            ("arbitrary", "arbitrary"),
            pipelined=[_nbytes((tm, d), F32), _nbytes((tm, dc), BF16)],
            resident=[_nbytes((3 * dc, d), BF16)],
            temporaries=[_nbytes((tm, d), F32), _nbytes((tm, d), BF16), _nbytes((tm, 3 * dc), F32),
                         5 * _nbytes((tm, dc), F32)]),
        name="prompt_conv",
    )(x, g, sc, sh, wt_conv, conv_w)


def _qkv_kernel(x_ref, g_ref, sc_ref, sh_ref, w_ref, bf_ref, qt_ref, kt_ref, vt_ref, lft_ref, ktok_ref, *, scale):
    u = _norm_mod(x_ref[0], g_ref[...], sc_ref[0], sh_ref[0]).astype(BF16)
    da = qt_ref.shape[1]
    r = _dot_nt(w_ref[...], u)
    qt_ref[0] = (r[:da] * scale).astype(BF16)
    kt = r[da:2 * da]
    kt_ref[0] = kt
    ktok_ref[0] = kt.T.astype(BF16)
    vt_ref[0] = r[2 * da:3 * da]
    lft_ref[0] = jax.nn.log_sigmoid(r[3 * da:] + bf_ref[...])


def _prompt_qkv(x, g, sc, sh, wt_qkvf, bf_col, da, scale, tm=512):
    b, t, d = x.shape
    n_heads = bf_col.shape[0]
    feat = lambda i, j: (i, 0, j)
    return pl.pallas_call(
        functools.partial(_qkv_kernel, scale=scale),
        grid=(b, t // tm),
        in_specs=[
            pl.BlockSpec((1, tm, d), lambda i, j: (i, j, 0)),
            _resident((1, d), lambda i, j: (0, 0)),
            _mod_spec(sc, tm),
            _mod_spec(sh, tm),
            _resident((3 * da + n_heads, d), lambda i, j: (0, 0)),
            _resident((n_heads, 1), lambda i, j: (0, 0)),
        ],
        out_specs=[
            pl.BlockSpec((1, da, tm), feat),
            pl.BlockSpec((1, da, tm), feat),
            pl.BlockSpec((1, da, tm), feat),
            pl.BlockSpec((1, n_heads, tm), feat),
            pl.BlockSpec((1, tm, da), lambda i, j: (i, j, 0)),
        ],
        out_shape=[
            jax.ShapeDtypeStruct((b, da, t), BF16),
            jax.ShapeDtypeStruct((b, da, t), F32),
            jax.ShapeDtypeStruct((b, da, t), F32),
            jax.ShapeDtypeStruct((b, n_heads, t), F32),
            jax.ShapeDtypeStruct((b, t, da), BF16),
        ],
        compiler_params=_params(
            ("arbitrary", "arbitrary"),
            pipelined=[_nbytes((tm, d), F32), 2 * _nbytes((da, tm), BF16), 2 * _nbytes((da, tm), F32)],
            resident=[_nbytes((3 * da + n_heads, d), BF16)],
            temporaries=[_nbytes((tm, d), F32), _nbytes((tm, d), BF16), _nbytes((3 * da + n_heads, tm), F32),
                         2 * _nbytes((tm, da), F32)]),
        name="prompt_qkv",
    )(x, g, sc, sh, wt_qkvf, bf_col)


def _split3(x):
    hi = x.astype(BF16)
    r = x - hi.astype(F32)
    mid = r.astype(BF16)
    lo = (r - mid.astype(F32)).astype(BF16)
    return hi, mid, lo


def _decay_bias_kernel(x_ref, o_ref):
    x = x_ref[0]
    n_heads, t = x.shape
    lane = lax.broadcasted_iota(jnp.int32, x.shape, 1)
    sh = 1
    while sh < t:
        x = x + jnp.where(lane >= sh, pltpu.roll(x, sh, 1), 0.0)
        sh *= 2
    pieces = _split3(x * (-LOG2E))
    rows = [p[h:h + 1].astype(F32) for h in range(n_heads) for p in pieces]
    rows.append(jnp.zeros((LANES - 3 * n_heads, t), F32))
    o_ref[0] = jnp.concatenate(rows, axis=0).T.astype(BF16)


def _prompt_decay_bias(lft):
    b, n_heads, t = lft.shape
    assert 3 * n_heads <= LANES
    return pl.pallas_call(
        _decay_bias_kernel,
        grid=(b,),
        in_specs=[pl.BlockSpec((1, n_heads, t), lambda i: (i, 0, 0))],
        out_specs=pl.BlockSpec((1, t, LANES), lambda i: (i, 0, 0)),
        out_shape=jax.ShapeDtypeStruct((b, t, LANES), BF16),
        name="prompt_decay_bias",
    )(lft)


def _paged_logits(first, q_ref, kn_ref, vn_ref, lfn_ref, kt_refs, lft_refs,
                  m_ref, l_ref, acc_ref, carry_ref, *, n_heads, hd):
    pages = len(kt_refs)
    da = q_ref.shape[-1]
    psz = kt_refs[0].shape[-1]
    hrow = lax.broadcasted_iota(jnp.int32, (n_heads, da), 0)
    ccol = lax.broadcasted_iota(jnp.int32, (n_heads, da), 1)
    own = (ccol >= hrow * hd) & (ccol < (hrow + 1) * hd)
    qmat = jnp.where(own, q_ref[0], 0.0)
    qmat_bf = qmat.astype(BF16)
    cn = lfn_ref[0]
    m_old = jnp.where(first, jnp.sum(qmat * kn_ref[0], axis=1, keepdims=True), m_ref[...])
    l_old = jnp.where(first, 1.0, l_ref[...])
    acc_old = jnp.where(first, jnp.broadcast_to(vn_ref[0], acc_ref.shape), acc_ref[...])
    carry = jnp.where(first, 0.0, carry_ref[...])

    lane = lax.broadcasted_iota(jnp.int32, (n_heads, psz), 1)
    logits = [None] * pages
    for r in reversed(range(pages)):
        s = _dot(qmat_bf, kt_refs[r][0].astype(BF16))
        lft = lft_refs[r][0]
        suf = jnp.where(lane < psz - 1, pltpu.roll(lft, psz - 1, 1), 0.0)
        sh = 1
        while sh < psz:
            suf = suf + jnp.where(lane < psz - sh, pltpu.roll(suf, psz - sh, 1), 0.0)
            sh *= 2
        logits[r] = s + (cn + carry) + suf
        carry = carry + suf[:, 0:1] + lft[:, 0:1]
    carry_ref[...] = carry
    lg = jnp.concatenate(logits, axis=1)
    return own, lg, m_old, l_old, acc_old


def _paged_update(state, vt_refs, o_ref, m_ref, l_ref, acc_ref):
    own, lg, m_old, l_old, acc_old = state
    m_new = jnp.maximum(m_old, jnp.max(lg, axis=1, keepdims=True))
    alpha = jnp.exp(m_old - m_new)
    p = jnp.exp(lg - m_new)
    l_new = alpha * l_old + jnp.sum(p, axis=1, keepdims=True)
    vt_all = jnp.concatenate([ref[0].astype(BF16) for ref in vt_refs], axis=1)
    acc_new = alpha * acc_old + _dot_nt(p.astype(BF16), vt_all)
    m_ref[...] = m_new
    l_ref[...] = l_new
    acc_ref[...] = acc_new
    o_ref[0] = jnp.sum(jnp.where(own, acc_new / l_new, 0.0), axis=0, keepdims=True).astype(o_ref.dtype)


def _attn_kernel(pt_ref, qta_ref, qtb_ref, k_ref, cb_ref, vt_ref, q_ref, kn_ref, vn_ref, lfn_ref, *rest,
                 hd, n_heads, pages, groups):
    kt_refs = rest[:pages]
    vts_refs = rest[pages:2 * pages]
    lft_refs = rest[2 * pages:3 * pages]
    o_ref, os_ref, vt_sc, m_ref, l_ref, acc_ref, carry_ref = rest[3 * pages:]
    del pt_ref
    hp = pl.program_id(1)
    qj = pl.program_id(2)
    step = (pl.program_id(0) * pl.num_programs(1) + hp) * pl.num_programs(2) + qj
    first = step % groups == 0
    nh = LANES // hd
    t = k_ref.shape[1]
    tq = qta_ref.shape[2]
    nq = t // tq

    def sampled_logits():
        return _paged_logits(first, q_ref, kn_ref, vn_ref, lfn_ref, kt_refs, lft_refs,
                             m_ref, l_ref, acc_ref, carry_ref, n_heads=n_heads, hd=hd)

    def sampled_update(state):
        _paged_update(state, vts_refs, os_ref, m_ref, l_ref, acc_ref)

    def own(rows, hh):
        return (rows >= hh * hd) & (rows < (hh + 1) * hd)

    def ones_row(hh):
        return ((hh + 1) % nh) * hd

    @pl.when(qj == 0)
    def _():
        vt = vt_ref[0].astype(BF16)
        frow = lax.broadcasted_iota(jnp.int32, vt.shape, 0)
        for hh in range(nh):
            vh = jnp.where(own(frow, hh), vt, jnp.zeros_like(vt))
            vt_sc[hh] = jnp.where(frow == ones_row(hh), jnp.ones_like(vt), vh)

    frow = lax.broadcasted_iota(jnp.int32, (LANES, tq), 0)
    causal = lax.broadcasted_iota(jnp.int32, (tq, tq), 0) <= lax.broadcasted_iota(jnp.int32, (tq, tq), 1)

    def queries(qt_ref):
        qt = qt_ref[0]
        out = []
        for hh in range(nh):
            h = hp * nh + hh
            qh = jnp.where(own(frow, hh), qt, jnp.zeros_like(qt))
            pick = jnp.where((frow >= 3 * h) & (frow < 3 * h + 3), 1.0, 0.0).astype(BF16)
            out.append(jnp.concatenate([qh, pick], axis=0))
        return out

    def keys(lo, hi):
        return jnp.concatenate([k_ref[0, lo:hi, :], cb_ref[0, lo:hi, :]], axis=1)

    def scores(c, qts):
        top = c * tq
        k_diag = keys(top, top + tq)
        s_diag = [jnp.where(causal, _dot(k_diag, qts[hh]), NEG_INF) for hh in range(nh)]
        s_top = [_dot(keys(0, top), qts[hh]) for hh in range(nh)] if c > 0 else None
        return s_diag, s_top

    def finish(c, s_diag, s_top, slot):
        top = c * tq
        ot = None
        for hh in range(nh):
            m = jnp.max(s_diag[hh], axis=0, keepdims=True)
            if c > 0:
                m = jnp.maximum(m, jnp.max(s_top[hh], axis=0, keepdims=True))
            acc = _dot(vt_sc[hh, :, top:top + tq], jnp.exp2(s_diag[hh] - m).astype(BF16))
            if c > 0:
                acc = acc + _dot(vt_sc[hh, :, 0:top], jnp.exp2(s_top[hh] - m).astype(BF16))
            r = ones_row(hh)
            oth = acc / acc[r:r + 1]
            ot = oth if ot is None else jnp.where(own(frow, hh), oth, ot)
        o_ref[0, slot * tq:(slot + 1) * tq, :] = ot.T.astype(BF16)

    for j in range(nq // 2):
        @pl.when(qj == j)
        def _(j=j):
            ca, cb = j, nq - 1 - j
            state = sampled_logits()
            sa = scores(ca, queries(qta_ref))
            sb = scores(cb, queries(qtb_ref))
            sampled_update(state)
            finish(ca, *sa, 0)
            finish(cb, *sb, 1)


def _paired_row_block(j, tm, tq, nq):
    per = tq // tm
    c = j // per
    pair = jnp.minimum(c, nq - 1 - c)
    slot = (c > pair).astype(jnp.int32)
    return (2 * pair + slot) * per + j % per


def _attention(qt, ktok, cbias, vt, q_s, k_new, v_new, lf_new_t, cache_kt, cache_vt, cache_lft, page_table,
               hd, tq=512):
    b, da, t = qt.shape
    ns = q_s.shape[0]
    n_heads = lf_new_t.shape[1]
    n_pages = page_table.shape[1]
    psz = cache_kt.shape[-1]
    nh = LANES // hd
    nq = t // tq
    assert nq % 2 == 0, "query tiles are processed in (j, last - j) pairs"
    grid = (b, da // LANES, nq // 2)
    steps = grid[0] * grid[1] * grid[2]
    assert steps % ns == 0 and n_pages % (steps // ns) == 0, "sampled tokens must tile the prompt grid"
    groups = steps // ns
    pages = n_pages // groups

    def step_of(n, h, i):
        return (n * grid[1] + h) * grid[2] + i

    def page_map(r):
        def index(n, h, i, pt):
            s = step_of(n, h, i)
            return (pt[s // groups, (groups - 1 - s % groups) * pages + r], 0, 0)
        return index

    tok = pl.BlockSpec((1, 1, da), lambda n, h, i, pt: (step_of(n, h, i) // groups, 0, 0))
    grid_spec = pltpu.PrefetchScalarGridSpec(
        num_scalar_prefetch=1,
        grid=grid,
        in_specs=[
            pl.BlockSpec((1, LANES, tq), lambda n, h, i, pt: (n, h, i)),
            pl.BlockSpec((1, LANES, tq), lambda n, h, i, pt: (n, h, nq - 1 - i)),
            pl.BlockSpec((1, t, LANES), lambda n, h, i, pt: (n, 0, h)),
            pl.BlockSpec((1, t, LANES), lambda n, h, i, pt: (n, 0, 0)),
            pl.BlockSpec((1, LANES, t), lambda n, h, i, pt: (n, h, 0)),
            tok, tok, tok,
            pl.BlockSpec((1, n_heads, 1), lambda n, h, i, pt: (step_of(n, h, i) // groups, 0, 0)),
        ]
        + [pl.BlockSpec((1, da, psz), page_map(r)) for r in range(pages)]
        + [pl.BlockSpec((1, da, psz), page_map(r)) for r in range(pages)]
        + [pl.BlockSpec((1, n_heads, psz), page_map(r)) for r in range(pages)],
        out_specs=[pl.BlockSpec((1, 2 * tq, LANES), lambda n, h, i, pt: (n, i, h)), tok],
        scratch_shapes=[
            pltpu.VMEM((nh, LANES, t), BF16),
            pltpu.VMEM((n_heads, 1), F32),
            pltpu.VMEM((n_heads, 1), F32),
            pltpu.VMEM((n_heads, da), F32),
            pltpu.VMEM((n_heads, 1), F32),
        ],
    )
    return pl.pallas_call(
        functools.partial(_attn_kernel, hd=hd, n_heads=n_heads, pages=pages, groups=groups),
        grid_spec=grid_spec,
        out_shape=[jax.ShapeDtypeStruct((b, t, da), BF16), jax.ShapeDtypeStruct((ns, 1, da), BF16)],
        compiler_params=_params(
            ("arbitrary", "arbitrary", "arbitrary"),
            pipelined=[2 * _nbytes((LANES, tq), BF16), 2 * _nbytes((t, LANES), BF16), _nbytes((LANES, t), F32),
                       2 * pages * _nbytes((da, psz), F32), pages * _nbytes((n_heads, LANES), F32),
                       _nbytes((2 * tq, LANES), BF16)],
            resident=[_nbytes((nh, LANES, t), BF16), _nbytes((n_heads, da), F32)],
            temporaries=[nh * _nbytes(((nq + 1) * tq, tq), F32), nh * _nbytes(((nq + 1) * tq, tq), BF16),
                         pages * _nbytes((da, psz), BF16), _nbytes((da, psz), BF16)]),
        name="attention",
    )(page_table, qt, qt, ktok, cbias, vt, q_s, k_new, v_new, lf_new_t,
      *([cache_kt] * pages), *([cache_vt] * pages), *([cache_lft] * pages))


def _mix_kernel(x_ref, a_ref, b_ref, g1_ref, sca_ref, sha_ref, gta_ref, g2_ref, scm_ref, shm_ref,
                wg_ref, wo_ref, h_ref, m_ref, *, tc):
    x = x_ref[0]
    d = x.shape[1]
    dc = a_ref.shape[-1]
    u = _norm_mod(x, g1_ref[...], sca_ref[0], sha_ref[0]).astype(BF16)
    a = a_ref[0]
    b = b_ref[0]
    gta = gta_ref[0]
    for c in range(d // tc):
        lo, hi = c * tc, (c + 1) * tc
        ga = _dot_nt(u, wg_ref[lo:hi, :])
        gb = _dot_nt(u, wg_ref[d + lo:d + hi, :])
        pa = _dot(a, wo_ref[:dc, lo:hi])
        pb = _dot(b, wo_ref[dc:, lo:hi])
        mixed = jax.nn.sigmoid(ga) * pa + jax.nn.sigmoid(gb) * pb
        h_ref[0, :, lo:hi] = x[:, lo:hi] + gta[:, lo:hi] * mixed
    m_ref[0] = _norm_mod(h_ref[0], g2_ref[...], scm_ref[0], shm_ref[0]).astype(BF16)


def _mix(x, a, bb, g1, sca, sha, gta, g2, scm, shm, wt_gate, w_out_bf, tm, tc=512, b_row_block=None):
    b, t, d = x.shape
    dc = a.shape[-1]
    da = bb.shape[-1]
    row = lambda i, j: (i, j, 0)
    brow = row if b_row_block is None else (lambda i, j: (i, b_row_block(j), 0))
    const = lambda i, j: (0, 0)
    return pl.pallas_call(
        functools.partial(_mix_kernel, tc=tc),
        grid=(b, t // tm),
        in_specs=[
            pl.BlockSpec((1, tm, d), row),
            pl.BlockSpec((1, tm, dc), row),
            pl.BlockSpec((1, tm, da), brow),
            _resident((1, d), const),
            _mod_spec(sca, tm), _mod_spec(sha, tm), _mod_spec(gta, tm),
            _resident((1, d), const),
            _mod_spec(scm, tm), _mod_spec(shm, tm),
            _resident((2 * d, d), const),
            _resident((dc + da, d), const),
        ],
        out_specs=[pl.BlockSpec((1, tm, d), row), pl.BlockSpec((1, tm, d), row)],
        out_shape=[jax.ShapeDtypeStruct((b, t, d), F32), jax.ShapeDtypeStruct((b, t, d), BF16)],
        compiler_params=_params(
            ("arbitrary", "arbitrary"),
            pipelined=[2 * _nbytes((tm, d), F32), _nbytes((tm, dc), BF16), _nbytes((tm, da), BF16),
                       _nbytes((tm, d), BF16), 5 * _nbytes((sca.shape[1], d), F32)],
            resident=[_nbytes((2 * d, d), BF16), _nbytes((dc + da, d), BF16)],
            temporaries=[2 * _nbytes((tm, d), F32), _nbytes((tm, d), BF16), 5 * _nbytes((tm, tc), F32)]),
        name="mix",
    )(x, a, bb, g1, sca, sha, gta, g2, scm, shm, wt_gate, w_out_bf)


def _mlp_kernel(m_ref, h_ref, gt_ref, gf_ref, wu_ref, wd_ref, y_ref, acc_ref):
    f = pl.program_id(2)

    @pl.when(f == 0)
    def _():
        acc_ref[...] = jnp.zeros_like(acc_ref)

    hid = _dot(m_ref[0], wu_ref[...])
    hid = jnp.square(jnp.maximum(hid, 0.0)).astype(BF16)
    acc_ref[...] += _dot(hid, wd_ref[...])

    @pl.when(f == pl.num_programs(2) - 1)
    def _():
        h2 = h_ref[0] + gt_ref[0] * acc_ref[...]
        y_ref[0] = _rmsnorm(h2, gf_ref[...])


def _mlp(m, h, gtm, g_final, w_up_bf, w_down_bf, tm, tf=1024):
    b, t, d = h.shape
    dff = w_up_bf.shape[1]
    row = lambda i, j, f: (i, j, 0)
    return pl.pallas_call(
        _mlp_kernel,
        grid=(b, t // tm, dff // tf),
        in_specs=[
            pl.BlockSpec((1, tm, d), row),
            pl.BlockSpec((1, tm, d), row),
            _mod_spec(gtm, tm),
            _resident((1, d), lambda i, j, f: (0, 0)),
            pl.BlockSpec((d, tf), lambda i, j, f: (0, f)),
            pl.BlockSpec((tf, d), lambda i, j, f: (f, 0)),
        ],
        out_specs=pl.BlockSpec((1, tm, d), row),
        out_shape=jax.ShapeDtypeStruct((b, t, d), F32),
        scratch_shapes=[pltpu.VMEM((tm, d), F32)],
        compiler_params=_params(
            ("arbitrary", "arbitrary", "arbitrary"),
            pipelined=[_nbytes((tm, d), BF16), 2 * _nbytes((tm, d), F32), _nbytes((d, tf), BF16),
                       _nbytes((tf, d), BF16), _nbytes((gtm.shape[1], d), F32)],
            resident=[_nbytes((tm, d), F32)],
            temporaries=[_nbytes((tm, tf), F32), _nbytes((tm, tf), BF16), 2 * _nbytes((tm, d), F32)]),
        name="mlp",
    )(m, h, gtm, g_final, w_up_bf, w_down_bf)


def _sample_proj_kernel(x_ref, g_ref, sc_ref, sh_ref, wb_ref, wc_ref, wx_ref, wq_ref, wk_ref, wv_ref,
                        wf_ref, bf_ref, h0_ref, h1_ref, cw_ref,
                        a_ref, z_ref, q_ref, k_ref, v_ref, lf_ref, *, scale):
    u = _norm_mod(x_ref[...], g_ref[...], sc_ref[...], sh_ref[...]).astype(BF16)
    cb = _dot_nt(u, wb_ref[...])
    z = _dot_nt(u, wc_ref[...]) * _dot_nt(u, wx_ref[...])
    cw = cw_ref[...]
    y = cw[0:1] * h0_ref[...] + cw[1:2] * h1_ref[...] + cw[2:3] * z
    a_ref[...] = (cb * y).astype(BF16)
    z_ref[...] = z
    q_ref[...] = _dot_nt(u, wq_ref[...]) * scale
    k_ref[...] = _dot_nt(u, wk_ref[...])
    v_ref[...] = _dot_nt(u, wv_ref[...])

    @pl.when(pl.program_id(0) == 0)
    def _():
        lf_ref[...] = jax.nn.log_sigmoid(_dot_nt(u, wf_ref[...]) + bf_ref[...])


def _sample_proj(x, g, sc, sh, wt_cqkv, wt_f, bf_row, hist0, hist1, conv_w, scale, tc=512):
    n, d = x.shape
    dc = conv_w.shape[1]
    n_heads = wt_f.shape[0]
    nc = dc // tc
    whole = lambda shape: pl.BlockSpec(shape, lambda c: (0, 0))
    cols = pl.BlockSpec((n, tc), lambda c: (0, c))
    wspec = lambda r: pl.BlockSpec((tc, d), lambda c, r=r: (r * nc + c, 0))
    return pl.pallas_call(
        functools.partial(_sample_proj_kernel, scale=scale),
        grid=(nc,),
        in_specs=[whole((n, d)), whole((1, d)), whole((n, d)), whole((n, d))]
        + [wspec(r) for r in range(6)]
        + [whole((n_heads, d)), whole((1, n_heads)), cols, cols, pl.BlockSpec((3, tc), lambda c: (0, c))],
        out_specs=[cols, cols, cols, cols, cols, whole((n, n_heads))],
        out_shape=[jax.ShapeDtypeStruct((n, dc), BF16)]
        + [jax.ShapeDtypeStruct((n, dc), F32)] * 4
        + [jax.ShapeDtypeStruct((n, n_heads), F32)],
        compiler_params=_params(
            ("arbitrary",),
            pipelined=[3 * _nbytes((n, d), F32), 6 * _nbytes((tc, d), BF16), 8 * _nbytes((n, tc), F32)],
            temporaries=[2 * _nbytes((n, d), F32), 6 * _nbytes((n, tc), F32)]),
        name="sample_proj",
    )(x, g, sc, sh, *([wt_cqkv] * 6), wt_f, bf_row, hist0, hist1, conv_w)


def kernel(x_prompt, x_sample, c_prompt, c_sample, cache_k, cache_v, cache_logf, state_conv, page_table,
           w_ada, b_ada, g_mix, w_in, b_f, conv_w, w_out, g_mlp, w_up, w_down, g_final):
    depth = w_ada.shape[0]
    assert depth == 1, "single-layer trunk"
    nb, seq, d = x_prompt.shape
    ns, dec_seq, _ = x_sample.shape
    assert dec_seq == 1, "one new token per sampled sequence"
    n_heads = b_f.shape[-1]
    hd = cache_k.shape[-1]
    da = n_heads * hd
    dc = conv_w.shape[-1]
    n_pool, psz = cache_k.shape[1], cache_k.shape[2]
    scale = hd ** -0.5
    assert LANES % hd == 0 and w_in.shape[-1] == 3 * dc + 3 * da + n_heads + 2 * d

    wt = jnp.transpose(w_in[0]).astype(BF16)
    o_q, o_f, o_g = 3 * dc, 3 * dc + 3 * da, 3 * dc + 3 * da + n_heads
    wt_qkvf = wt[o_q:o_g]
    wt_f = wt[o_f:o_g]
    wt_gate = wt[o_g:]
    w_out_bf = w_out[0].astype(BF16)
    w_up_bf = w_up[0].astype(BF16)
    w_down_bf = w_down[0].astype(BF16)
    g1 = g_mix[0].reshape(1, d)
    g2 = g_mlp[0].reshape(1, d)
    gf = g_final.reshape(1, d)
    cw = conv_w[0]

    mod = _modulation(jnp.concatenate([c_prompt, c_sample], axis=0), w_ada[0], b_ada[0])
    mod_p = [m.reshape(nb, 1, d) for m in jnp.split(mod[:nb], 6, axis=-1)]
    mod_s = [m.reshape(1, ns, d) for m in jnp.split(mod[nb:], 6, axis=-1)]

    sh_a, sc_a, gt_a, sh_m, sc_m, gt_m = mod_p
    a_p, conv_p = _prompt_conv(x_prompt, g1, sc_a, sh_a, wt, cw)
    qt_p, kt_p, vt_p, lft_p, ktok_p = _prompt_qkv(
        x_prompt, g1, sc_a, sh_a, wt_qkvf, b_f[0].reshape(n_heads, 1), da, scale * LOG2E)
    ssh_a, ssc_a, sgt_a, ssh_m, ssc_m, sgt_m = mod_s
    xs = x_sample.reshape(ns, d)
    hist = state_conv[0]
    a_s, z_s, q_s, k_s, v_s, lf_s = _sample_proj(
        xs, g1, ssc_a[0], ssh_a[0], wt, wt_f, b_f[0].reshape(1, n_heads), hist[:, 0], hist[:, 1], cw, scale)

    cache_kt = jnp.transpose(cache_k[0], (0, 2, 3, 1)).reshape(n_pool, da, psz)
    cache_vt = jnp.transpose(cache_v[0], (0, 2, 3, 1)).reshape(n_pool, da, psz)
    cache_lft = jnp.transpose(cache_logf[0], (0, 2, 1))
    b_p, b_s = _attention(
        qt_p, ktok_p, _prompt_decay_bias(lft_p), vt_p,
        q_s.reshape(ns, 1, da), k_s.reshape(ns, 1, da), v_s.reshape(ns, 1, da), lf_s.reshape(ns, n_heads, 1),
        cache_kt, cache_vt, cache_lft, page_table, hd, tq=ATTN_TQ)

    h_p, m_p = _mix(x_prompt, a_p, b_p, g1, sc_a, sh_a, gt_a, g2, sc_m, sh_m, wt_gate, w_out_bf, tm=MIX_TM,
                    b_row_block=functools.partial(_paired_row_block, tm=MIX_TM, tq=ATTN_TQ, nq=seq // ATTN_TQ))
    y_prompt = _mlp(m_p, h_p, gt_m, gf, w_up_bf, w_down_bf, tm=MLP_TM)
    xs3 = xs.reshape(1, ns, d)
    h_s, m_s = _mix(xs3, a_s.reshape(1, ns, dc), b_s.reshape(1, ns, da), g1, ssc_a, ssh_a, sgt_a, g2, ssc_m, ssh_m,
                    wt_gate, w_out_bf, tm=ns)
    y_sample = _mlp(m_s, h_s, sgt_m, gf, w_up_bf, w_down_bf, tm=ns).reshape(ns, 1, d)

    def heads_last(xt):
        return jnp.transpose(xt.reshape(nb, n_heads, hd, seq), (0, 3, 1, 2))[None]

    return (
        y_prompt,
        y_sample,
        heads_last(kt_p),
        heads_last(vt_p),
        jnp.transpose(lft_p, (0, 2, 1))[None],
        conv_p.reshape(1, nb, 2, dc),
        k_s.reshape(1, ns, 1, n_heads, hd),
        v_s.reshape(1, ns, 1, n_heads, hd),
        lf_s.reshape(1, ns, 1, n_heads),
        jnp.stack([hist[:, 1], z_s], axis=1).reshape(1, ns, 2, dc),
    )
```

```python
import functools
import math

import jax
import jax.numpy as jnp
from jax import lax
from jax.experimental import pallas as pl
from jax.experimental.pallas import tpu as pltpu

F32 = jnp.float32
BF16 = jnp.bfloat16
NORM_EPS = 1e-6
LANES = 128
MIB = 1024 * 1024
V7X_VMEM_BYTES = 64 * MIB
NEG_INF = float("-inf")
LOG2E = 1.4426950408889634
ATTN_TQ = 512
MIX_TM = 256
MLP_TM = 512


def _nbytes(shape, dtype):
    return math.prod(shape) * jnp.dtype(dtype).itemsize


def _params(semantics, pipelined=(), resident=(), temporaries=()):
    need = 2 * sum(pipelined) + sum(resident) + sum(temporaries)
    assert need <= V7X_VMEM_BYTES, f"VMEM estimate {need / MIB:.1f} MiB exceeds the TensorCore's VMEM"
    return pltpu.CompilerParams(vmem_limit_bytes=int(need), dimension_semantics=semantics)


def _dot(a, b):
    return jnp.dot(a, b, preferred_element_type=F32)


def _dot_nt(a, b):
    return lax.dot_general(a, b, (((1,), (1,)), ((), ())), preferred_element_type=F32)


def _rmsnorm(x, g):
    return (x * lax.rsqrt(jnp.mean(x * x, axis=-1, keepdims=True) + NORM_EPS)) * g


def _norm_mod(x, g, sc, sh):
    return _rmsnorm(x, g) * (1.0 + sc) + sh


def _resident(shape, index_map):
    return pl.BlockSpec(shape, index_map, pipeline_mode=pl.Buffered(1))


def _mod_spec(mod, tm):
    d = mod.shape[-1]
    if mod.shape[1] == 1:
        return pl.BlockSpec((1, 1, d), lambda b, t, *_: (b, 0, 0))
    return pl.BlockSpec((1, tm, d), lambda b, t, *_: (b, t, 0))


def _mod_kernel(c_ref, w_ref, b_ref, o_ref):
    c = c_ref[...]
    s = (c * jax.nn.sigmoid(c)).astype(BF16)
    o_ref[...] = _dot(s, w_ref[...].astype(BF16)) + b_ref[...]


def _modulation(c, w_ada, b_ada, tn=1024):
    n, d = c.shape
    dn = w_ada.shape[1]
    return pl.pallas_call(
        _mod_kernel,
        grid=(dn // tn,),
        in_specs=[
            pl.BlockSpec((n, d), lambda j: (0, 0)),
            pl.BlockSpec((d, tn), lambda j: (0, j)),
            pl.BlockSpec((1, tn), lambda j: (0, j)),
        ],
        out_specs=pl.BlockSpec((n, tn), lambda j: (0, j)),
        out_shape=jax.ShapeDtypeStruct((n, dn), F32),
        compiler_params=_params(
            ("arbitrary",),
            pipelined=[_nbytes((n, d), F32), _nbytes((d, tn), F32), _nbytes((n, tn), F32)],
            temporaries=[_nbytes((d, tn), BF16), _nbytes((n, d), F32)]),
        name="modulation",
    )(c, w_ada, b_ada.reshape(1, dn))


def _proj_kernel(x_ref, g_ref, sc_ref, sh_ref, w_ref, wq_ref, cw_ref, bf_ref,
                 a_ref, st_ref, qt_ref, kt_ref, vt_ref, lft_ref, ktok_ref, carry_ref, *, scale):
    t = pl.program_id(1)

    @pl.when(t == 0)
    def _():
        carry_ref[...] = jnp.zeros_like(carry_ref)

    u = _norm_mod(x_ref[0], g_ref[...], sc_ref[0], sh_ref[0]).astype(BF16)
    dc = cw_ref.shape[1]
    da = qt_ref.shape[1]
    r = _dot_nt(u, w_ref[...])
    rq = _dot_nt(wq_ref[...], u)
    cb, cc, cx = r[:, :dc], r[:, dc:2 * dc], r[:, 2 * dc:]
    z = cc * cx
    tm = z.shape[0]
    prev = carry_ref[...]
    row = lax.broadcasted_iota(jnp.int32, z.shape, 0)
    z1 = jnp.where(row >= 1, pltpu.roll(z, 1, 0), prev[7:8])
    z2 = jnp.where(row >= 2, pltpu.roll(z, 2, 0), jnp.where(row == 1, prev[7:8], prev[6:7]))
    cw = cw_ref[...]
    y = cw[0:1] * z2 + cw[1:2] * z1 + cw[2:3] * z
    a_ref[0] = (cb * y).astype(BF16)
    carry_ref[...] = z[tm - 8:tm]
    st_ref[0] = z[tm - 2:tm]
    qt_ref[0] = (rq[:da] * scale).astype(BF16)
    kt = rq[da:2 * da]
    kt_ref[0] = kt
    ktok_ref[0] = kt.T.astype(BF16)
    vt_ref[0] = rq[2 * da:3 * da]
    lft_ref[0] = jax.nn.log_sigmoid(rq[3 * da:] + bf_ref[...])


def _prompt_proj(x, g, sc, sh, wt, wt_qkvf, conv_w, bf_col, da, scale, tm=512):
    b, t, d = x.shape
    dc = conv_w.shape[1]
    n_heads = bf_col.shape[0]
    rows = lambda i, j: (i, j, 0)
    feat = lambda i, j: (i, 0, j)
    const = lambda i, j: (0, 0)
    return pl.pallas_call(
        functools.partial(_proj_kernel, scale=scale),
        grid=(b, t // tm),
        in_specs=[
            pl.BlockSpec((1, tm, d), rows),
            _resident((1, d), const),
            _mod_spec(sc, tm),
            _mod_spec(sh, tm),
            _resident((3 * dc, d), const),
            _resident((3 * da + n_heads, d), const),
            _resident((3, dc), const),
            _resident((n_heads, 1), const),
        ],
        out_specs=[
            pl.BlockSpec((1, tm, dc), rows),
            pl.BlockSpec((1, 2, dc), lambda i, j: (i, 0, 0)),
            pl.BlockSpec((1, da, tm), feat),
            pl.BlockSpec((1, da, tm), feat),
            pl.BlockSpec((1, da, tm), feat),
            pl.BlockSpec((1, n_heads, tm), feat),
            pl.BlockSpec((1, tm, da), rows),
        ],
        out_shape=[
            jax.ShapeDtypeStruct((b, t, dc), BF16),
            jax.ShapeDtypeStruct((b, 2, dc), F32),
            jax.ShapeDtypeStruct((b, da, t), BF16),
            jax.ShapeDtypeStruct((b, da, t), F32),
            jax.ShapeDtypeStruct((b, da, t), F32),
            jax.ShapeDtypeStruct((b, n_heads, t), F32),
            jax.ShapeDtypeStruct((b, t, da), BF16),
        ],
        scratch_shapes=[pltpu.VMEM((8, dc), F32)],
        compiler_params=_params(
            ("arbitrary", "arbitrary"),
            pipelined=[_nbytes((tm, d), F32), _nbytes((tm, dc), BF16), 2 * _nbytes((da, tm), BF16),
                       2 * _nbytes((da, tm), F32)],
            resident=[_nbytes((3 * dc, d), BF16), _nbytes((3 * da + n_heads, d), BF16)],
            temporaries=[_nbytes((tm, 3 * dc), F32), _nbytes((3 * da + n_heads, tm), F32)]),
        name="prompt_proj",
    )(x, g, sc, sh, wt, wt_qkvf, conv_w, bf_col)


def _split3(x):
    hi = x.astype(BF16)
    r = x - hi.astype(F32)
    mid = r.astype(BF16)
    lo = (r - mid.astype(F32)).astype(BF16)
    return hi, mid, lo


def _decay_bias_kernel(x_ref, o_ref):
    x = x_ref[0]
    n_heads, t = x.shape
    lane = lax.broadcasted_iota(jnp.int32, x.shape, 1)
    sh = 1
    while sh < t:
        x = x + jnp.where(lane >= sh, pltpu.roll(x, sh, 1), 0.0)
        sh *= 2
    pieces = _split3(x * (-LOG2E))
    rows = [p[h:h + 1].astype(F32) for h in range(n_heads) for p in pieces]
    rows.append(jnp.zeros((LANES - 3 * n_heads, t), F32))
    o_ref[0] = jnp.concatenate(rows, axis=0).T.astype(BF16)


def _prompt_decay_bias(lft):
    b, n_heads, t = lft.shape
    assert 3 * n_heads <= LANES
    return pl.pallas_call(
        _decay_bias_kernel,
        grid=(b,),
        in_specs=[pl.BlockSpec((1, n_heads, t), lambda i: (i, 0, 0))],
        out_specs=pl.BlockSpec((1, t, LANES), lambda i: (i, 0, 0)),
        out_shape=jax.ShapeDtypeStruct((b, t, LANES), BF16),
        name="prompt_decay_bias",
    )(lft)


def _paged_logits(first, q_ref, kn_ref, vn_ref, lfn_ref, kt_refs, lft_refs,
                  m_ref, l_ref, acc_ref, carry_ref, *, n_heads, hd):
    pages = len(kt_refs)
    da = q_ref.shape[-1]
    psz = kt_refs[0].shape[-1]
    hrow = lax.broadcasted_iota(jnp.int32, (n_heads, da), 0)
    ccol = lax.broadcasted_iota(jnp.int32, (n_heads, da), 1)
    own = (ccol >= hrow * hd) & (ccol < (hrow + 1) * hd)
    qmat = jnp.where(own, q_ref[0], 0.0)
    qmat_bf = qmat.astype(BF16)
    cn = lfn_ref[0]
    m_old = jnp.where(first, jnp.sum(qmat * kn_ref[0], axis=1, keepdims=True), m_ref[...])
    l_old = jnp.where(first, 1.0, l_ref[...])
    acc_old = jnp.where(first, jnp.broadcast_to(vn_ref[0], acc_ref.shape), acc_ref[...])
    carry = jnp.where(first, 0.0, carry_ref[...])

    lane = lax.broadcasted_iota(jnp.int32, (n_heads, psz), 1)
    logits = [None] * pages
    for r in reversed(range(pages)):
        s = _dot(qmat_bf, kt_refs[r][0].astype(BF16))
        lft = lft_refs[r][0]
        suf = jnp.where(lane < psz - 1, pltpu.roll(lft, psz - 1, 1), 0.0)
        sh = 1
        while sh < psz:
            suf = suf + jnp.where(lane < psz - sh, pltpu.roll(suf, psz - sh, 1), 0.0)
            sh *= 2
        logits[r] = s + (cn + carry) + suf
        carry = carry + suf[:, 0:1] + lft[:, 0:1]
    carry_ref[...] = carry
    lg = jnp.concatenate(logits, axis=1)
    return own, lg, m_old, l_old, acc_old


def _paged_update(state, vt_refs, o_ref, m_ref, l_ref, acc_ref):
    own, lg, m_old, l_old, acc_old = state
    m_new = jnp.maximum(m_old, jnp.max(lg, axis=1, keepdims=True))
    alpha = jnp.exp(m_old - m_new)
    p = jnp.exp(lg - m_new)
    l_new = alpha * l_old + jnp.sum(p, axis=1, keepdims=True)
    vt_all = jnp.concatenate([ref[0].astype(BF16) for ref in vt_refs], axis=1)
    acc_new = alpha * acc_old + _dot_nt(p.astype(BF16), vt_all)
    m_ref[...] = m_new
    l_ref[...] = l_new
    acc_ref[...] = acc_new
    o_ref[0] = jnp.sum(jnp.where(own, acc_new / l_new, 0.0), axis=0, keepdims=True).astype(o_ref.dtype)


def _attn_kernel(pt_ref, qta_ref, qtb_ref, k_ref, cb_ref, vt_ref, q_ref, kn_ref, vn_ref, lfn_ref, *rest,
                 hd, n_heads, pages, groups):
    kt_refs = rest[:pages]
    vts_refs = rest[pages:2 * pages]
    lft_refs = rest[2 * pages:3 * pages]
    o_ref, os_ref, vt_sc, m_ref, l_ref, acc_ref, carry_ref = rest[3 * pages:]
    del pt_ref
    hp = pl.program_id(1)
    qj = pl.program_id(2)
    step = (pl.program_id(0) * pl.num_programs(1) + hp) * pl.num_programs(2) + qj
    first = step % groups == 0
    nh = LANES // hd
    t = k_ref.shape[1]
    tq = qta_ref.shape[2]
    nq = t // tq

    def sampled_logits():
        return _paged_logits(first, q_ref, kn_ref, vn_ref, lfn_ref, kt_refs, lft_refs,
                             m_ref, l_ref, acc_ref, carry_ref, n_heads=n_heads, hd=hd)

    def sampled_update(state):
        _paged_update(state, vts_refs, os_ref, m_ref, l_ref, acc_ref)

    def own(rows, hh):
        return (rows >= hh * hd) & (rows < (hh + 1) * hd)

    def ones_row(hh):
        return ((hh + 1) % nh) * hd

    @pl.when(qj == 0)
    def _():
        vt = vt_ref[0].astype(BF16)
        frow = lax.broadcasted_iota(jnp.int32, vt.shape, 0)
        for hh in range(nh):
            vh = jnp.where(own(frow, hh), vt, jnp.zeros_like(vt))
            vt_sc[hh] = jnp.where(frow == ones_row(hh), jnp.ones_like(vt), vh)

    frow = lax.broadcasted_iota(jnp.int32, (LANES, tq), 0)
    causal = lax.broadcasted_iota(jnp.int32, (tq, tq), 0) <= lax.broadcasted_iota(jnp.int32, (tq, tq), 1)

    def queries(qt_ref):
        qt = qt_ref[0]
        out = []
        for hh in range(nh):
            h = hp * nh + hh
            qh = jnp.where(own(frow, hh), qt, jnp.zeros_like(qt))
            pick = jnp.where((frow >= 3 * h) & (frow < 3 * h + 3), 1.0, 0.0).astype(BF16)
            out.append(jnp.concatenate([qh, pick], axis=0))
        return out

    def keys(lo, hi):
        return jnp.concatenate([k_ref[0, lo:hi, :], cb_ref[0, lo:hi, :]], axis=1)

    def scores(c, qts):
        top = c * tq
        k_diag = keys(top, top + tq)
        s_diag = [jnp.where(causal, _dot(k_diag, qts[hh]), NEG_INF) for hh in range(nh)]
        s_top = [_dot(keys(0, top), qts[hh]) for hh in range(nh)] if c > 0 else None
        return s_diag, s_top

    def finish(c, s_diag, s_top, slot):
        top = c * tq
        ot = None
        for hh in range(nh):
            m = jnp.max(s_diag[hh], axis=0, keepdims=True)
            if c > 0:
                m = jnp.maximum(m, jnp.max(s_top[hh], axis=0, keepdims=True))
            acc = _dot(vt_sc[hh, :, top:top + tq], jnp.exp2(s_diag[hh] - m).astype(BF16))
            if c > 0:
                acc = acc + _dot(vt_sc[hh, :, 0:top], jnp.exp2(s_top[hh] - m).astype(BF16))
            r = ones_row(hh)
            oth = acc / acc[r:r + 1]
            ot = oth if ot is None else jnp.where(own(frow, hh), oth, ot)
        o_ref[0, slot * tq:(slot + 1) * tq, :] = ot.T.astype(BF16)

    for j in range(nq // 2):
        @pl.when(qj == j)
        def _(j=j):
            ca, cb = j, nq - 1 - j
            state = sampled_logits()
            sa = scores(ca, queries(qta_ref))
            sb = scores(cb, queries(qtb_ref))
            sampled_update(state)
            finish(ca, *sa, 0)
            finish(cb, *sb, 1)


def _paired_row_block(j, tm, tq, nq):
    per = tq // tm
    c = j // per
    pair = jnp.minimum(c, nq - 1 - c)
    slot = (c > pair).astype(jnp.int32)
    return (2 * pair + slot) * per + j % per


def _attention(qt, ktok, cbias, vt, q_s, k_new, v_new, lf_new_t, cache_kt, cache_vt, cache_lft, page_table,
               hd, tq=512):
    b, da, t = qt.shape
    ns = q_s.shape[0]
    n_heads = lf_new_t.shape[1]
    n_pages = page_table.shape[1]
    psz = cache_kt.shape[-1]
    nh = LANES // hd
    nq = t // tq
    assert nq % 2 == 0, "query tiles are processed in (j, last - j) pairs"
    grid = (b, da // LANES, nq // 2)
    steps = grid[0] * grid[1] * grid[2]
    assert steps % ns == 0 and n_pages % (steps // ns) == 0, "sampled tokens must tile the prompt grid"
    groups = steps // ns
    pages = n_pages // groups

    def step_of(n, h, i):
        return (n * grid[1] + h) * grid[2] + i

    def page_map(r):
        def index(n, h, i, pt):
            s = step_of(n, h, i)
            return (pt[s // groups, (groups - 1 - s % groups) * pages + r], 0, 0)
        return index

    tok = pl.BlockSpec((1, 1, da), lambda n, h, i, pt: (step_of(n, h, i) // groups, 0, 0))
    grid_spec = pltpu.PrefetchScalarGridSpec(
        num_scalar_prefetch=1,
        grid=grid,
        in_specs=[
            pl.BlockSpec((1, LANES, tq), lambda n, h, i, pt: (n, h, i)),
            pl.BlockSpec((1, LANES, tq), lambda n, h, i, pt: (n, h, nq - 1 - i)),
            pl.BlockSpec((1, t, LANES), lambda n, h, i, pt: (n, 0, h)),
            pl.BlockSpec((1, t, LANES), lambda n, h, i, pt: (n, 0, 0)),
            pl.BlockSpec((1, LANES, t), lambda n, h, i, pt: (n, h, 0)),
            tok, tok, tok,
            pl.BlockSpec((1, n_heads, 1), lambda n, h, i, pt: (step_of(n, h, i) // groups, 0, 0)),
        ]
        + [pl.BlockSpec((1, da, psz), page_map(r)) for r in range(pages)]
        + [pl.BlockSpec((1, da, psz), page_map(r)) for r in range(pages)]
        + [pl.BlockSpec((1, n_heads, psz), page_map(r)) for r in range(pages)],
        out_specs=[pl.BlockSpec((1, 2 * tq, LANES), lambda n, h, i, pt: (n, i, h)), tok],
        scratch_shapes=[
            pltpu.VMEM((nh, LANES, t), BF16),
            pltpu.VMEM((n_heads, 1), F32),
            pltpu.VMEM((n_heads, 1), F32),
            pltpu.VMEM((n_heads, da), F32),
            pltpu.VMEM((n_heads, 1), F32),
        ],
    )
    return pl.pallas_call(
        functools.partial(_attn_kernel, hd=hd, n_heads=n_heads, pages=pages, groups=groups),
        grid_spec=grid_spec,
        out_shape=[jax.ShapeDtypeStruct((b, t, da), BF16), jax.ShapeDtypeStruct((ns, 1, da), BF16)],
        compiler_params=_params(
            ("arbitrary", "arbitrary", "arbitrary"),
            pipelined=[2 * _nbytes((LANES, tq), BF16), 2 * _nbytes((t, LANES), BF16), _nbytes((LANES, t), F32),
                       2 * pages * _nbytes((da, psz), F32), pages * _nbytes((n_heads, LANES), F32),
                       _nbytes((2 * tq, LANES), BF16)],
            resident=[_nbytes((nh, LANES, t), BF16), _nbytes((n_heads, da), F32)],
            temporaries=[nh * _nbytes(((nq + 1) * tq, tq), F32), nh * _nbytes(((nq + 1) * tq, tq), BF16),
                         pages * _nbytes((da, psz), BF16), _nbytes((da, psz), BF16)]),
        name="attention",
    )(page_table, qt, qt, ktok, cbias, vt, q_s, k_new, v_new, lf_new_t,
      *([cache_kt] * pages), *([cache_vt] * pages), *([cache_lft] * pages))


def _mix_kernel(x_ref, a_ref, b_ref, g1_ref, sca_ref, sha_ref, gta_ref, g2_ref, scm_ref, shm_ref,
                wg_ref, wo_ref, h_ref, m_ref, *, tc):
    x = x_ref[0]
    d = x.shape[1]
    dc = a_ref.shape[-1]
    u = _norm_mod(x, g1_ref[...], sca_ref[0], sha_ref[0]).astype(BF16)
    a = a_ref[0]
    b = b_ref[0]
    gta = gta_ref[0]
    for c in range(d // tc):
        lo, hi = c * tc, (c + 1) * tc
        ga = _dot_nt(u, wg_ref[lo:hi, :])
        gb = _dot_nt(u, wg_ref[d + lo:d + hi, :])
        pa = _dot(a, wo_ref[:dc, lo:hi])
        pb = _dot(b, wo_ref[dc:, lo:hi])
        mixed = jax.nn.sigmoid(ga) * pa + jax.nn.sigmoid(gb) * pb
        h_ref[0, :, lo:hi] = x[:, lo:hi] + gta[:, lo:hi] * mixed
    m_ref[0] = _norm_mod(h_ref[0], g2_ref[...], scm_ref[0], shm_ref[0]).astype(BF16)


def _mix(x, a, bb, g1, sca, sha, gta, g2, scm, shm, wt_gate, w_out_bf, tm, tc=512, b_row_block=None):
    b, t, d = x.shape
    dc = a.shape[-1]
    da = bb.shape[-1]
    row = lambda i, j: (i, j, 0)
    brow = row if b_row_block is None else (lambda i, j: (i, b_row_block(j), 0))
    const = lambda i, j: (0, 0)
    return pl.pallas_call(
        functools.partial(_mix_kernel, tc=tc),
        grid=(b, t // tm),
        in_specs=[
            pl.BlockSpec((1, tm, d), row),
            pl.BlockSpec((1, tm, dc), row),
            pl.BlockSpec((1, tm, da), brow),
            _resident((1, d), const),
            _mod_spec(sca, tm), _mod_spec(sha, tm), _mod_spec(gta, tm),
            _resident((1, d), const),
            _mod_spec(scm, tm), _mod_spec(shm, tm),
            _resident((2 * d, d), const),
            _resident((dc + da, d), const),
        ],
        out_specs=[pl.BlockSpec((1, tm, d), row), pl.BlockSpec((1, tm, d), row)],
        out_shape=[jax.ShapeDtypeStruct((b, t, d), F32), jax.ShapeDtypeStruct((b, t, d), BF16)],
        compiler_params=_params(
            ("arbitrary", "arbitrary"),
            pipelined=[2 * _nbytes((tm, d), F32), _nbytes((tm, dc), BF16), _nbytes((tm, da), BF16),
                       _nbytes((tm, d), BF16), 5 * _nbytes((sca.shape[1], d), F32)],
            resident=[_nbytes((2 * d, d), BF16), _nbytes((dc + da, d), BF16)],
            temporaries=[2 * _nbytes((tm, d), F32), _nbytes((tm, d), BF16), 5 * _nbytes((tm, tc), F32)]),
        name="mix",
    )(x, a, bb, g1, sca, sha, gta, g2, scm, shm, wt_gate, w_out_bf)


def _mlp_kernel(m_ref, h_ref, gt_ref, gf_ref, wu_ref, wd_ref, y_ref, acc_ref):
    f = pl.program_id(2)

    @pl.when(f == 0)
    def _():
        acc_ref[...] = jnp.zeros_like(acc_ref)

    hid = _dot(m_ref[0], wu_ref[...])
    hid = jnp.square(jnp.maximum(hid, 0.0)).astype(BF16)
    acc_ref[...] += _dot(hid, wd_ref[...])

    @pl.when(f == pl.num_programs(2) - 1)
    def _():
        h2 = h_ref[0] + gt_ref[0] * acc_ref[...]
        y_ref[0] = _rmsnorm(h2, gf_ref[...])


def _mlp(m, h, gtm, g_final, w_up_bf, w_down_bf, tm, tf=1024):
    b, t, d = h.shape
    dff = w_up_bf.shape[1]
    row = lambda i, j, f: (i, j, 0)
    return pl.pallas_call(
        _mlp_kernel,
        grid=(b, t // tm, dff // tf),
        in_specs=[
            pl.BlockSpec((1, tm, d), row),
            pl.BlockSpec((1, tm, d), row),
            _mod_spec(gtm, tm),
            _resident((1, d), lambda i, j, f: (0, 0)),
            pl.BlockSpec((d, tf), lambda i, j, f: (0, f)),
            pl.BlockSpec((tf, d), lambda i, j, f: (f, 0)),
        ],
        out_specs=pl.BlockSpec((1, tm, d), row),
        out_shape=jax.ShapeDtypeStruct((b, t, d), F32),
        scratch_shapes=[pltpu.VMEM((tm, d), F32)],
        compiler_params=_params(
            ("arbitrary", "arbitrary", "arbitrary"),
            pipelined=[_nbytes((tm, d), BF16), 2 * _nbytes((tm, d), F32), _nbytes((d, tf), BF16),
                       _nbytes((tf, d), BF16), _nbytes((gtm.shape[1], d), F32)],
            resident=[_nbytes((tm, d), F32)],
            temporaries=[_nbytes((tm, tf), F32), _nbytes((tm, tf), BF16), 2 * _nbytes((tm, d), F32)]),
        name="mlp",
    )(m, h, gtm, g_final, w_up_bf, w_down_bf)


def _sample_proj_kernel(x_ref, g_ref, sc_ref, sh_ref, wb_ref, wc_ref, wx_ref, wq_ref, wk_ref, wv_ref,
                        wf_ref, bf_ref, h0_ref, h1_ref, cw_ref,
                        a_ref, z_ref, q_ref, k_ref, v_ref, lf_ref, *, scale):
    u = _norm_mod(x_ref[...], g_ref[...], sc_ref[...], sh_ref[...]).astype(BF16)
    cb = _dot_nt(u, wb_ref[...])
    z = _dot_nt(u, wc_ref[...]) * _dot_nt(u, wx_ref[...])
    cw = cw_ref[...]
    y = cw[0:1] * h0_ref[...] + cw[1:2] * h1_ref[...] + cw[2:3] * z
    a_ref[...] = (cb * y).astype(BF16)
    z_ref[...] = z
    q_ref[...] = _dot_nt(u, wq_ref[...]) * scale
    k_ref[...] = _dot_nt(u, wk_ref[...])
    v_ref[...] = _dot_nt(u, wv_ref[...])

    @pl.when(pl.program_id(0) == 0)
    def _():
        lf_ref[...] = jax.nn.log_sigmoid(_dot_nt(u, wf_ref[...]) + bf_ref[...])


def _sample_proj(x, g, sc, sh, wt_cqkv, wt_f, bf_row, hist0, hist1, conv_w, scale, tc=512):
    n, d = x.shape
    dc = conv_w.shape[1]
    n_heads = wt_f.shape[0]
    nc = dc // tc
    whole = lambda shape: pl.BlockSpec(shape, lambda c: (0, 0))
    cols = pl.BlockSpec((n, tc), lambda c: (0, c))
    wspec = lambda r: pl.BlockSpec((tc, d), lambda c, r=r: (r * nc + c, 0))
    return pl.pallas_call(
        functools.partial(_sample_proj_kernel, scale=scale),
        grid=(nc,),
        in_specs=[whole((n, d)), whole((1, d)), whole((n, d)), whole((n, d))]
        + [wspec(r) for r in range(6)]
        + [whole((n_heads, d)), whole((1, n_heads)), cols, cols, pl.BlockSpec((3, tc), lambda c: (0, c))],
        out_specs=[cols, cols, cols, cols, cols, whole((n, n_heads))],
        out_shape=[jax.ShapeDtypeStruct((n, dc), BF16)]
        + [jax.ShapeDtypeStruct((n, dc), F32)] * 4
        + [jax.ShapeDtypeStruct((n, n_heads), F32)],
        compiler_params=_params(
            ("arbitrary",),
            pipelined=[3 * _nbytes((n, d), F32), 6 * _nbytes((tc, d), BF16), 8 * _nbytes((n, tc), F32)],
            temporaries=[2 * _nbytes((n, d), F32), 6 * _nbytes((n, tc), F32)]),
        name="sample_proj",
    )(x, g, sc, sh, *([wt_cqkv] * 6), wt_f, bf_row, hist0, hist1, conv_w)


def kernel(x_prompt, x_sample, c_prompt, c_sample, cache_k, cache_v, cache_logf, state_conv, page_table,
           w_ada, b_ada, g_mix, w_in, b_f, conv_w, w_out, g_mlp, w_up, w_down, g_final):
    depth = w_ada.shape[0]
    assert depth == 1, "single-layer trunk"
    nb, seq, d = x_prompt.shape
    ns, dec_seq, _ = x_sample.shape
    assert dec_seq == 1, "one new token per sampled sequence"
    n_heads = b_f.shape[-1]
    hd = cache_k.shape[-1]
    da = n_heads * hd
    dc = conv_w.shape[-1]
    n_pool, psz = cache_k.shape[1], cache_k.shape[2]
    scale = hd ** -0.5
    assert LANES % hd == 0 and w_in.shape[-1] == 3 * dc + 3 * da + n_heads + 2 * d

    wt = jnp.transpose(w_in[0]).astype(BF16)
    o_q, o_f, o_g = 3 * dc, 3 * dc + 3 * da, 3 * dc + 3 * da + n_heads
    wt_qkvf = wt[o_q:o_g]
    wt_f = wt[o_f:o_g]
    wt_gate = wt[o_g:]
    w_out_bf = w_out[0].astype(BF16)
    w_up_bf = w_up[0].astype(BF16)
    w_down_bf = w_down[0].astype(BF16)
    g1 = g_mix[0].reshape(1, d)
    g2 = g_mlp[0].reshape(1, d)
    gf = g_final.reshape(1, d)
    cw = conv_w[0]

    mod = _modulation(jnp.concatenate([c_prompt, c_sample], axis=0), w_ada[0], b_ada[0])
    mod_p = [m.reshape(nb, 1, d) for m in jnp.split(mod[:nb], 6, axis=-1)]
    mod_s = [m.reshape(1, ns, d) for m in jnp.split(mod[nb:], 6, axis=-1)]

    sh_a, sc_a, gt_a, sh_m, sc_m, gt_m = mod_p
    a_p, conv_p, qt_p, kt_p, vt_p, lft_p, ktok_p = _prompt_proj(
        x_prompt, g1, sc_a, sh_a, wt, wt_qkvf, cw, b_f[0].reshape(n_heads, 1), da, scale * LOG2E)
    ssh_a, ssc_a, sgt_a, ssh_m, ssc_m, sgt_m = mod_s
    xs = x_sample.reshape(ns, d)
    hist = state_conv[0]
    a_s, z_s, q_s, k_s, v_s, lf_s = _sample_proj(
        xs, g1, ssc_a[0], ssh_a[0], wt, wt_f, b_f[0].reshape(1, n_heads), hist[:, 0], hist[:, 1], cw, scale)

    cache_kt = jnp.transpose(cache_k[0], (0, 2, 3, 1)).reshape(n_pool, da, psz)
    cache_vt = jnp.transpose(cache_v[0], (0, 2, 3, 1)).reshape(n_pool, da, psz)
    cache_lft = jnp.transpose(cache_logf[0], (0, 2, 1))
    b_p, b_s = _attention(
        qt_p, ktok_p, _prompt_decay_bias(lft_p), vt_p,
        q_s.reshape(ns, 1, da), k_s.reshape(ns, 1, da), v_s.reshape(ns, 1, da), lf_s.reshape(ns, n_heads, 1),
        cache_kt, cache_vt, cache_lft, page_table, hd, tq=ATTN_TQ)

    h_p, m_p = _mix(x_prompt, a_p, b_p, g1, sc_a, sh_a, gt_a, g2, sc_m, sh_m, wt_gate, w_out_bf, tm=MIX_TM,
                    b_row_block=functools.partial(_paired_row_block, tm=MIX_TM, tq=ATTN_TQ, nq=seq // ATTN_TQ))
    y_prompt = _mlp(m_p, h_p, gt_m, gf, w_up_bf, w_down_bf, tm=MLP_TM)
    xs3 = xs.reshape(1, ns, d)
    h_s, m_s = _mix(xs3, a_s.reshape(1, ns, dc), b_s.reshape(1, ns, da), g1, ssc_a, ssh_a, sgt_a, g2, ssc_m, ssh_m,
                    wt_gate, w_out_bf, tm=ns)
    y_sample = _mlp(m_s, h_s, sgt_m, gf, w_up_bf, w_down_bf, tm=ns).reshape(ns, 1, d)

    def heads_last(xt):
        return jnp.transpose(xt.reshape(nb, n_heads, hd, seq), (0, 3, 1, 2))[None]

    return (
        y_prompt,
        y_sample,
        heads_last(kt_p),
        heads_last(vt_p),
        jnp.transpose(lft_p, (0, 2, 1))[None],
        conv_p.reshape(1, nb, 2, dc),
        k_s.reshape(1, ns, 1, n_heads, hd),
        v_s.reshape(1, ns, 1, n_heads, hd),
        lf_s.reshape(1, ns, 1, n_heads),
        jnp.stack([hist[:, 1], z_s], axis=1).reshape(1, ns, 2, dc),
    )
```
